```python
import math
import jax
import jax.numpy as jnp
from jax import lax
import numpy as np

D_MODEL = 1024
BATCH = 8
SEQ = 4096
DEPTH = 4

N_MIXERS = 2
GDN_HEADS = 8
GDN_DK = 128
GDN_DV = 128
GDN_CONV = 4
GDN_CHUNK = 64
GDN_IN = 2 * GDN_HEADS * GDN_DK + 2 * GDN_HEADS * GDN_DV + 2 * GDN_HEADS
DSA_HEADS = 16
DSA_QK_DIM = 64
DSA_V_DIM = 64
DSA_Q_LORA = 256
DSA_KV_LORA = 256
IDX_HEADS = 8
IDX_DIM = 64
TOPK_MAX = 256
Q_BLOCK = 128
DSA_IN = DSA_Q_LORA + DSA_KV_LORA + IDX_DIM + IDX_HEADS
FFN_HIDDEN = -(-8 * D_MODEL // (3 * 256)) * 256
NORM_EPS = 1e-6
L2_EPS = 1e-6

kernel_name = "hybrid_gdn_dsa_sandwich_adaln_trunk"


def rms_norm(x, g):
    xf = x.astype(jnp.float32)
    y = xf * lax.rsqrt(jnp.mean(xf * xf, axis=-1, keepdims=True) + NORM_EPS)
    return y.astype(x.dtype) * g


def layer_norm(x, g, b):
    xf = x.astype(jnp.float32)
    mu = jnp.mean(xf, axis=-1, keepdims=True)
    var = jnp.mean(jnp.square(xf - mu), axis=-1, keepdims=True)
    return ((xf - mu) * lax.rsqrt(var + NORM_EPS)).astype(x.dtype) * g + b


def l2_normalize(x):
    return x * lax.rsqrt(jnp.sum(x * x, axis=-1, keepdims=True) + L2_EPS)


def causal_dwconv(u, w):
    K, C = w.shape
    return lax.conv_general_dilated(
        u.astype(jnp.float32), w.astype(jnp.float32)[:, None, :],
        window_strides=(1,), padding=[(K - 1, 0)],
        dimension_numbers=('NWC', 'WIO', 'NWC'), feature_group_count=C)


def chunk_gated_delta_rule(q, k, v, g, beta):
    B, S, H, DK = q.shape
    DV = v.shape[-1]
    C = GDN_CHUNK
    N = S // C

    def chunks(t):
        t = t.reshape(B, N, C, H, *t.shape[3:])
        return jnp.moveaxis(t, (1, 3), (0, 2))

    q, k, v, g, beta = chunks(q), chunks(k), chunks(v), chunks(g), chunks(beta)
    G = jnp.cumsum(g, axis=-1)
    pos = jnp.arange(C)
    lower = pos[:, None] >= pos[None, :]
    strict = pos[:, None] > pos[None, :]
    decay = jnp.exp(jnp.where(lower, G[..., :, None] - G[..., None, :], -jnp.inf))
    kb = k * beta[..., None]
    L = jnp.where(strict, jnp.einsum('nbhid,nbhjd->nbhij', kb, k) * decay, 0.0)
    A = L + jnp.eye(C, dtype=L.dtype)
    u = lax.linalg.triangular_solve(A, v * beta[..., None], left_side=True, lower=True)
    w = lax.linalg.triangular_solve(A, kb * jnp.exp(G)[..., None], left_side=True, lower=True)
    attn = jnp.einsum('nbhid,nbhjd->nbhij', q, k) * decay
    g_last = G[..., -1]
    q_g = q * jnp.exp(G)[..., None]
    k_g = k * jnp.exp(g_last[..., None] - G)[..., None]

    def step(state, xs):
        u_c, w_c, qg_c, kg_c, attn_c, gl_c = xs
        v_new = u_c - jnp.einsum('bhck,bhkv->bhcv', w_c, state)
        o_c = (jnp.einsum('bhck,bhkv->bhcv', qg_c, state)
               + jnp.einsum('bhij,bhjv->bhiv', attn_c, v_new))
        state = (state * jnp.exp(gl_c)[..., None, None]
                 + jnp.einsum('bhck,bhcv->bhkv', kg_c, v_new))
        return state, o_c

    state0 = jnp.zeros((B, H, DK, DV), jnp.float32)
    _, o = lax.scan(step, state0, (u, w, q_g, k_g, attn, g_last))
    return jnp.moveaxis(o, (0, 2), (1, 3)).reshape(B, S, H, DV)


def gdn_mixer(h, w_in, conv_w, a_log, dt_bias, norm_g, w_out):
    B, S, _ = h.shape
    H = GDN_HEADS
    HK = GDN_HEADS * GDN_DK
    HV = GDN_HEADS * GDN_DV
    proj = h @ w_in
    qkv, z, b_raw, a_raw = jnp.split(proj, [2 * HK + HV, 2 * HK + 2 * HV, 2 * HK + 2 * HV + H], axis=-1)
    qkv = jax.nn.silu(causal_dwconv(qkv, conv_w))
    q, k, v = jnp.split(qkv, [HK, 2 * HK], axis=-1)
    q = l2_normalize(q.reshape(B, S, H, GDN_DK)) * (GDN_DK ** -0.5)
    k = l2_normalize(k.reshape(B, S, H, GDN_DK))
    v = v.reshape(B, S, H, GDN_DV)
    beta = jax.nn.sigmoid(b_raw.astype(jnp.float32))
    g = -jnp.exp(a_log.astype(jnp.float32)) * jax.nn.softplus(
        a_raw.astype(jnp.float32) + dt_bias.astype(jnp.float32))
    o = chunk_gated_delta_rule(q, k, v, g, beta)
    o = rms_norm(o, norm_g) * jax.nn.silu(z.reshape(B, S, H, GDN_DV).astype(jnp.float32))
    return o.reshape(B, S, HV).astype(h.dtype) @ w_out


def dsa_mixer(h, w_in, q_norm_g, kv_norm_g, w_uq, w_uk, w_uv, w_qidx, kidx_ln_g, kidx_ln_b, w_out):
    B, S, _ = h.shape
    topk = min(TOPK_MAX, S // 4)
    proj = h @ w_in
    c_q, c_kv, k_idx, w_idx = jnp.split(
        proj, [DSA_Q_LORA, DSA_Q_LORA + DSA_KV_LORA, DSA_Q_LORA + DSA_KV_LORA + IDX_DIM], axis=-1)
    c_q = rms_norm(c_q, q_norm_g)
    c_kv = rms_norm(c_kv, kv_norm_g)
    q = (c_q @ w_uq).reshape(B, S, DSA_HEADS, DSA_QK_DIM)
    q_idx = (c_q @ w_qidx).reshape(B, S, IDX_HEADS, IDX_DIM)
    k_idx = layer_norm(k_idx, kidx_ln_g, kidx_ln_b)
    w_idx = w_idx * ((IDX_HEADS ** -0.5) * (IDX_DIM ** -0.5))
    nb = S // Q_BLOCK

    def blocks(t):
        return jnp.swapaxes(t.reshape(B, nb, Q_BLOCK, *t.shape[2:]), 0, 1)

    starts = jnp.arange(nb, dtype=jnp.int32) * Q_BLOCK
    key_pos = jnp.arange(S, dtype=jnp.int32)

    def attend_block(args):
        q_b, qi_b, wi_b, t0 = args
        t = t0 + jnp.arange(Q_BLOCK, dtype=jnp.int32)
        logits = jnp.einsum('bqhd,bsd->bqhs', qi_b, k_idx).astype(jnp.float32)
        score = jnp.einsum('bqh,bqhs->bqs', wi_b.astype(jnp.float32), jax.nn.relu(logits))
        score = jnp.where(key_pos[None, None, :] <= t[None, :, None], score, -jnp.inf)
        _, idx = lax.top_k(score, topk)
        valid = idx <= t[None, :, None]
        kv_sel = jax.vmap(lambda kv, i: kv[i])(c_kv, idx)
        q_lat = jnp.einsum('bqhd,hdc->bqhc', q_b, w_uk)
        s = jnp.einsum('bqhc,bqkc->bqhk', q_lat, kv_sel).astype(jnp.float32) * (DSA_QK_DIM ** -0.5)
        s = jnp.where(valid[:, :, None, :], s, -jnp.inf)
        p = jax.nn.softmax(s, axis=-1).astype(kv_sel.dtype)
        o_lat = jnp.einsum('bqhk,bqkc->bqhc', p, kv_sel)
        o = jnp.einsum('bqhc,hcv->bqhv', o_lat, w_uv)
        return o.reshape(B, Q_BLOCK, DSA_HEADS * DSA_V_DIM)

    o = lax.map(attend_block, (blocks(q), blocks(q_idx), blocks(w_idx), starts))
    o = jnp.swapaxes(o, 0, 1).reshape(B, S, DSA_HEADS * DSA_V_DIM)
    return o @ w_out


def swiglu(h, w_in, w_out):
    gate, up = jnp.split(h @ w_in, 2, axis=-1)
    return (jax.nn.silu(gate) * up) @ w_out


def _normal(key, shape, fan_in, gain=1.0):
    return jax.random.normal(key, shape, jnp.float32) * (gain * fan_in ** -0.5)


def setup_inputs(seed: int = 0) -> dict:
    key = jax.random.key(seed)
    ks = jax.random.split(key, 23)
    D = D_MODEL
    n_gdn = (DEPTH + N_MIXERS - 1) // N_MIXERS
    n_dsa = DEPTH // N_MIXERS
    HK = GDN_HEADS * GDN_DK
    HV = GDN_HEADS * GDN_DV
    nrm = lambda k, s: jax.random.normal(k, s, jnp.float32)
    x = nrm(ks[0], (BATCH, SEQ, D))
    c = nrm(ks[1], (BATCH, D))
    ada_w = _normal(ks[2], (DEPTH, D, 6 * D), D, 0.5)
    ada_b = 0.02 * nrm(ks[3], (DEPTH, 6 * D))
    norm_g = 1.0 + 0.05 * nrm(ks[4], (DEPTH, 4, D))
    ffn_w_in = _normal(ks[5], (DEPTH, D, 2 * FFN_HIDDEN), D)
    ffn_w_out = _normal(ks[6], (DEPTH, FFN_HIDDEN, D), FFN_HIDDEN)
    gdn_w_in = _normal(ks[7], (n_gdn, D, GDN_IN), D)
    gdn_conv_w = _normal(ks[8], (n_gdn, GDN_CONV, 2 * HK + HV), GDN_CONV)
    gdn_a_log = jnp.log(jax.random.uniform(ks[9], (n_gdn, GDN_HEADS), jnp.float32, 1.0, 16.0))
    dt = jnp.exp(jax.random.uniform(ks[10], (n_gdn, GDN_HEADS), jnp.float32,
                                    math.log(1e-3), math.log(1e-1)))
    gdn_dt_bias = dt + jnp.log(-jnp.expm1(-dt))
    gdn_norm_g = 1.0 + 0.05 * nrm(ks[11], (n_gdn, GDN_DV))
    gdn_w_out = _normal(ks[12], (n_gdn, HV, D), HV)
    dsa_w_in = _normal(ks[13], (n_dsa, D, DSA_IN), D)
    dsa_q_norm_g = 1.0 + 0.05 * nrm(ks[14], (n_dsa, DSA_Q_LORA))
    dsa_kv_norm_g = 1.0 + 0.05 * nrm(ks[15], (n_dsa, DSA_KV_LORA))
    dsa_w_uq = _normal(ks[16], (n_dsa, DSA_Q_LORA, DSA_HEADS * DSA_QK_DIM), DSA_Q_LORA)
    dsa_w_uk = _normal(ks[17], (n_dsa, DSA_HEADS, DSA_QK_DIM, DSA_KV_LORA), DSA_KV_LORA)
    dsa_w_uv = _normal(ks[18], (n_dsa, DSA_HEADS, DSA_KV_LORA, DSA_V_DIM), DSA_KV_LORA)
    dsa_w_qidx = _normal(ks[19], (n_dsa, DSA_Q_LORA, IDX_HEADS * IDX_DIM), DSA_Q_LORA)
    dsa_kidx_ln_g = 1.0 + 0.05 * nrm(ks[20], (n_dsa, IDX_DIM))
    dsa_kidx_ln_b = 0.02 * nrm(ks[21], (n_dsa, IDX_DIM))
    dsa_w_out = _normal(ks[22], (n_dsa, DSA_HEADS * DSA_V_DIM, D), DSA_HEADS * DSA_V_DIM)
    return {"x": x, "c": c, "ada_w": ada_w, "ada_b": ada_b, "norm_g": norm_g,
            "ffn_w_in": ffn_w_in, "ffn_w_out": ffn_w_out,
            "gdn_w_in": gdn_w_in, "gdn_conv_w": gdn_conv_w, "gdn_a_log": gdn_a_log,
            "gdn_dt_bias": gdn_dt_bias, "gdn_norm_g": gdn_norm_g, "gdn_w_out": gdn_w_out,
            "dsa_w_in": dsa_w_in, "dsa_q_norm_g": dsa_q_norm_g, "dsa_kv_norm_g": dsa_kv_norm_g,
            "dsa_w_uq": dsa_w_uq, "dsa_w_uk": dsa_w_uk, "dsa_w_uv": dsa_w_uv,
            "dsa_w_qidx": dsa_w_qidx, "dsa_kidx_ln_g": dsa_kidx_ln_g,
            "dsa_kidx_ln_b": dsa_kidx_ln_b, "dsa_w_out": dsa_w_out}


def reference(x, c, ada_w, ada_b, norm_g, ffn_w_in, ffn_w_out,
              gdn_w_in, gdn_conv_w, gdn_a_log, gdn_dt_bias, gdn_norm_g, gdn_w_out,
              dsa_w_in, dsa_q_norm_g, dsa_kv_norm_g, dsa_w_uq, dsa_w_uk, dsa_w_uv,
              dsa_w_qidx, dsa_kidx_ln_g, dsa_kidx_ln_b, dsa_w_out):
    c_act = jax.nn.silu(c)
    for layer in range(DEPTH):
        mod = c_act @ ada_w[layer] + ada_b[layer]
        sh_m, sc_m, ga_m, sh_f, sc_f, ga_f = [m[:, None, :] for m in jnp.split(mod, 6, axis=-1)]
        h = rms_norm(x, norm_g[layer, 0]) * (1.0 + sc_m) + sh_m
        j = layer // N_MIXERS
        if layer % N_MIXERS == 0:
            y = gdn_mixer(h, gdn_w_in[j], gdn_conv_w[j], gdn_a_log[j], gdn_dt_bias[j],
                          gdn_norm_g[j], gdn_w_out[j])
        else:
            y = dsa_mixer(h, dsa_w_in[j], dsa_q_norm_g[j], dsa_kv_norm_g[j], dsa_w_uq[j],
                          dsa_w_uk[j], dsa_w_uv[j], dsa_w_qidx[j], dsa_kidx_ln_g[j],
                          dsa_kidx_ln_b[j], dsa_w_out[j])
        x = x + (1.0 + ga_m) * rms_norm(y, norm_g[layer, 1])
        h = rms_norm(x, norm_g[layer, 2]) * (1.0 + sc_f) + sh_f
        y = swiglu(h, ffn_w_in[layer], ffn_w_out[layer])
        x = x + (1.0 + ga_f) * rms_norm(y, norm_g[layer, 3])
    return x
```

```python
import functools

import jax
import jax.numpy as jnp
from jax import lax
from jax.experimental import pallas as pl
from jax.experimental.pallas import tpu as pltpu

NORM_EPS = 1e-6
L2_EPS = 1e-6

GDN_HEADS = 8
GDN_DK = 128
GDN_DV = 128
GDN_CONV = 4
GDN_CHUNK = 64

DSA_HEADS = 16
DSA_QK_DIM = 64
DSA_V_DIM = 64
DSA_Q_LORA = 256
DSA_KV_LORA = 256
IDX_HEADS = 8
IDX_DIM = 64
TOPK_MAX = 256
Q_BLOCK = 128

LANES = 128
VMEM_LIMIT = 56 * 1024 * 1024
NEG_BIG = -1e30
INT_MIN = -(2 ** 31)

F32 = jnp.float32
BF16 = jnp.bfloat16
HIGHEST = lax.Precision.HIGHEST


def _cparams(*sem):
    return pltpu.CompilerParams(dimension_semantics=sem, vmem_limit_bytes=VMEM_LIMIT)


def _dot(a, b, precision=None):
    return jnp.dot(a, b, preferred_element_type=F32, precision=precision)


def _dot_nt(a, b, precision=None):
    return lax.dot_general(a, b, (((1,), (1,)), ((), ())),
                           preferred_element_type=F32, precision=precision)


def _dot_tn(a, b, precision=None):
    return lax.dot_general(a, b, (((0,), (0,)), ((), ())),
                           preferred_element_type=F32, precision=precision)


def _silu(x):
    return x * jax.nn.sigmoid(x)


def _rms(x, g):
    return x * lax.rsqrt(jnp.mean(x * x, axis=-1, keepdims=True) + NORM_EPS) * g


def _const_spec(shape):
    zeros = (0,) * len(shape)
    return pl.BlockSpec(shape, lambda *_: zeros)


def _mod_kernel(c_ref, w_ref, b_ref, o_ref):
    c = c_ref[...]
    o_ref[0] = _dot(_silu(c), w_ref[0], precision=HIGHEST) + b_ref[0]


def _adaln_mod(c, ada_w, ada_b):
    depth, d, n = ada_w.shape
    b = c.shape[0]
    tn = n // 4
    return pl.pallas_call(
        _mod_kernel,
        grid=(depth, n // tn),
        in_specs=[
            pl.BlockSpec((b, d), lambda l, j: (0, 0)),
            pl.BlockSpec((1, d, tn), lambda l, j: (l, 0, j)),
            pl.BlockSpec((1, 1, tn), lambda l, j: (l, 0, j)),
        ],
        out_specs=pl.BlockSpec((1, b, tn), lambda l, j: (l, 0, j)),
        out_shape=jax.ShapeDtypeStruct((depth, b, n), F32),
        compiler_params=_cparams("parallel", "parallel"),
        name="adaln_mod",
    )(c, ada_w, ada_b.reshape(depth, 1, n))


def _norm_mod(x, g, scale, shift):
    return _rms(x, g) * (1.0 + scale) + shift


def _ffn_kernel(x_ref, mod_ref, ng_ref, wg_ref, wu_ref, wo_ref, o_ref, h_scr, acc_scr):
    x = x_ref[0]
    m = mod_ref[0]
    ng = ng_ref[...]
    h_scr[...] = _norm_mod(x, ng[2:3], m[4:5], m[3:4]).astype(BF16)
    acc_scr[...] = jnp.zeros_like(acc_scr)

    def body(j, carry):
        h = h_scr[...]
        gate = _dot(h, wg_ref[j])
        up = _dot(h, wu_ref[j])
        a = (_silu(gate) * up).astype(BF16)
        acc_scr[...] += _dot(a, wo_ref[j])
        return carry

    lax.fori_loop(0, wg_ref.shape[0], body, 0)
    o_ref[0] = x + (1.0 + m[5:6]) * _rms(acc_scr[...], ng[3:4])


def _ffn(x, mod, ng, wg, wu, wo, tm):
    b, s, d = x.shape
    return pl.pallas_call(
        _ffn_kernel,
        grid=(b, s // tm),
        in_specs=[
            pl.BlockSpec((1, tm, d), lambda i, j: (i, j, 0)),
            pl.BlockSpec((1, 6, d), lambda i, j: (i, 0, 0)),
            _const_spec(ng.shape),
            _const_spec(wg.shape),
            _const_spec(wu.shape),
            _const_spec(wo.shape),
        ],
        out_specs=pl.BlockSpec((1, tm, d), lambda i, j: (i, j, 0)),
        out_shape=jax.ShapeDtypeStruct(x.shape, F32),
        scratch_shapes=[pltpu.VMEM((tm, d), BF16), pltpu.VMEM((tm, d), F32)],
        compiler_params=_cparams("parallel", "parallel"),
        name="ffn",
    )(x, mod, ng, wg, wu, wo)


def _outproj_kernel(o_ref, x_ref, mod_ref, ng_ref, w_ref, out_ref):
    m = mod_ref[0]
    y = _dot(o_ref[0], w_ref[...])
    out_ref[0] = x_ref[0] + (1.0 + m[2:3]) * _rms(y, ng_ref[1:2])


def _outproj(o, x, mod, ng, w, tm):
    b, s, d = x.shape
    return pl.pallas_call(
        _outproj_kernel,
        grid=(b, s // tm),
        in_specs=[
            pl.BlockSpec((1, tm, o.shape[-1]), lambda i, j: (i, j, 0)),
            pl.BlockSpec((1, tm, d), lambda i, j: (i, j, 0)),
            pl.BlockSpec((1, 6, d), lambda i, j: (i, 0, 0)),
            _const_spec(ng.shape),
            _const_spec(w.shape),
        ],
        out_specs=pl.BlockSpec((1, tm, d), lambda i, j: (i, j, 0)),
        out_shape=jax.ShapeDtypeStruct(x.shape, F32),
        compiler_params=_cparams("parallel", "parallel"),
        name="outproj",
    )(o, x, mod, ng, w)


def _gdn_in_kernel(x_ref, mod_ref, ng_ref, w_ref, wgate_ref, alog_ref, dtb_ref,
                   proj_ref, gates_ref, h_scr):
    x = x_ref[0]
    m = mod_ref[0]
    tm = x.shape[0]
    h_scr[...] = _norm_mod(x, ng_ref[0:1], m[1:2], m[0:1]).astype(BF16)
    h = h_scr[...]
    n = w_ref.shape[1]
    tn = 512
    for j in range(n // tn):
        proj_ref[0, :, j * tn:(j + 1) * tn] = _dot(h, w_ref[:, j * tn:(j + 1) * tn]).astype(BF16)
    raw = _dot(h, wgate_ref[...])
    nh = raw.shape[1] // 2
    beta = jax.nn.sigmoid(raw[:, :nh])
    a = raw[:, nh:] + dtb_ref[...]
    softplus = jnp.maximum(a, 0.0) + jnp.log1p(jnp.exp(-jnp.abs(a)))
    g = -jnp.exp(alog_ref[...]) * softplus
    r = lax.broadcasted_iota(jnp.int32, (tm, tm), 0)
    c = lax.broadcasted_iota(jnp.int32, (tm, tm), 1)
    shift = GDN_CHUNK.bit_length() - 1
    same_chunk = jnp.right_shift(r, shift) == jnp.right_shift(c, shift)
    mcum = jnp.where(same_chunk, jnp.where(r >= c, 1.0, 0.0), 0.0)
    gcum = _dot(mcum, g, precision=HIGHEST)
    gates_ref[0, :, 0:nh] = beta
    gates_ref[0, :, nh:2 * nh] = gcum


def _gdn_in(x, mod, ng, w, wgate, a_log, dt_bias, tm):
    b, s, d = x.shape
    n = w.shape[1]
    nh2 = wgate.shape[1]
    return pl.pallas_call(
        _gdn_in_kernel,
        grid=(b, s // tm),
        in_specs=[
            pl.BlockSpec((1, tm, d), lambda i, j: (i, j, 0)),
            pl.BlockSpec((1, 6, d), lambda i, j: (i, 0, 0)),
            _const_spec(ng.shape),
            _const_spec(w.shape),
            _const_spec(wgate.shape),
            _const_spec(a_log.shape),
            _const_spec(dt_bias.shape),
        ],
        out_specs=[
            pl.BlockSpec((1, tm, n), lambda i, j: (i, j, 0)),
            pl.BlockSpec((1, tm, nh2), lambda i, j: (i, j, 0)),
        ],
        out_shape=[
            jax.ShapeDtypeStruct((b, s, n), BF16),
            jax.ShapeDtypeStruct((b, s, nh2), F32),
        ],
        scratch_shapes=[pltpu.VMEM((tm, d), BF16)],
        compiler_params=_cparams("parallel", "parallel"),
        name="gdn_in",
    )(x, mod, ng, w, wgate, a_log, dt_bias)


def _unit_lower_inverse(low):
    c = low.shape[0]
    eye = jnp.where(lax.broadcasted_iota(jnp.int32, (c, c), 0)
                    == lax.broadcasted_iota(jnp.int32, (c, c), 1), 1.0, 0.0)
    p = -low
    x = eye + p
    steps = max(1, (c - 1).bit_length()) - 1
    for _ in range(steps):
        p = _dot(p, p, precision=HIGHEST)
        x = x + _dot(x, p, precision=HIGHEST)
    return x


def _gdn_kernel(proj_ref, gcol_ref, grow_ref, convw_ref, ng_ref, out_ref,
                conv_scr, carry_scr, state_scr):
    tb = proj_ref.shape[1]
    nh = GDN_HEADS
    dk = GDN_DK
    cs = GDN_CHUNK
    hk = nh * dk
    nqkv = conv_scr.shape[1]

    @pl.when(pl.program_id(1) == 0)
    def _():
        carry_scr[...] = jnp.zeros_like(carry_scr)
        state_scr[...] = jnp.zeros_like(state_scr)

    cw = 512
    for j in range(nqkv // cw):
        cols = slice(j * cw, (j + 1) * cw)
        raw = proj_ref[0, :, cols].astype(F32)
        ext = jnp.concatenate([carry_scr[:, cols], raw], axis=0)
        w = convw_ref[:, cols]
        y = w[0:1] * ext[5:5 + tb]
        for tap in range(1, GDN_CONV):
            y = y + w[tap:tap + 1] * ext[5 + tap:5 + tap + tb]
        conv_scr[:, cols] = _silu(y)
        carry_scr[:, cols] = raw[tb - 8:tb]

    ri = lax.broadcasted_iota(jnp.int32, (cs, cs), 0)
    ci = lax.broadcasted_iota(jnp.int32, (cs, cs), 1)
    lower = ri >= ci
    strict = ri > ci
    ng = ng_ref[...]

    def chunk_body(c, carry):
        r0 = pl.multiple_of(c * cs, cs)
        rows = pl.ds(r0, cs)
        gc = gcol_ref[0, rows, :]
        gr = grow_ref[0, c]
        for h in range(nh):
            q = conv_scr[rows, h * dk:(h + 1) * dk]
            k = conv_scr[rows, hk + h * dk:hk + (h + 1) * dk]
            v = conv_scr[rows, 2 * hk + h * dk:2 * hk + (h + 1) * dk]
            q = q * lax.rsqrt(jnp.sum(q * q, axis=-1, keepdims=True) + L2_EPS) * (dk ** -0.5)
            k = k * lax.rsqrt(jnp.sum(k * k, axis=-1, keepdims=True) + L2_EPS)
            beta = gc[:, h:h + 1]
            gcum = gc[:, nh + h:nh + h + 1]
            grow = gr[nh + h:nh + h + 1, :]
            glast = gcum[cs - 1:cs, :]
            decay = jnp.where(lower, jnp.exp(jnp.minimum(gcum - grow, 0.0)), 0.0)
            eg = jnp.exp(gcum)
            kb = k * beta
            kbf = k.astype(BF16)
            kk = _dot_nt(kb.astype(BF16), kbf)
            tinv = _unit_lower_inverse(jnp.where(strict, kk * decay, 0.0))
            u = _dot(tinv, v * beta, precision=HIGHEST)
            w = _dot(tinv, kb * eg, precision=HIGHEST)
            attn = jnp.where(lower, _dot_nt(q.astype(BF16), kbf) * decay, 0.0)
            st = state_scr[h]
            stb = st.astype(BF16)
            v_new = u - _dot(w.astype(BF16), stb)
            o = _dot((q * eg).astype(BF16), stb) + _dot(attn.astype(BF16), v_new.astype(BF16))
            kg = k * jnp.exp(glast - gcum)
            state_scr[h] = st * jnp.exp(glast) + _dot_tn(kg.astype(BF16), v_new.astype(BF16))
            z = proj_ref[0, rows, 3 * hk + h * dk:3 * hk + (h + 1) * dk].astype(F32)
            out_ref[0, rows, h * dk:(h + 1) * dk] = (_rms(o, ng) * _silu(z)).astype(BF16)
        return carry

    lax.fori_loop(0, tb // cs, chunk_body, 0)


def _gdn_core(proj, gates, grow, conv_w, norm_g, tb):
    b, s, n = proj.shape
    hv = GDN_HEADS * GDN_DV
    nqkv = conv_w.shape[1]
    return pl.pallas_call(
        _gdn_kernel,
        grid=(b, s // tb),
        in_specs=[
            pl.BlockSpec((1, tb, n), lambda i, j: (i, j, 0)),
            pl.BlockSpec((1, tb, gates.shape[-1]), lambda i, j: (i, j, 0)),
            pl.BlockSpec((1, tb // GDN_CHUNK) + grow.shape[2:], lambda i, j: (i, j, 0, 0)),
            _const_spec(conv_w.shape),
            _const_spec(norm_g.shape),
        ],
        out_specs=pl.BlockSpec((1, tb, hv), lambda i, j: (i, j, 0)),
        out_shape=jax.ShapeDtypeStruct((b, s, hv), BF16),
        scratch_shapes=[
            pltpu.VMEM((tb, nqkv), F32),
            pltpu.VMEM((8, nqkv), F32),
            pltpu.VMEM((GDN_HEADS, GDN_DK, GDN_DV), F32),
        ],
        compiler_params=_cparams("parallel", "arbitrary"),
        name="gdn_core",
    )(proj, gates, grow, conv_w, norm_g)


def _dsa_in_kernel(x_ref, mod_ref, ng_ref, w_ref, qg_ref, kvg_ref, lng_ref, lnb_ref,
                   wuq_ref, wqi_ref, q_ref, qi_ref, ckv_ref, aux_ref):
    x = x_ref[0]
    m = mod_ref[0]
    h = _norm_mod(x, ng_ref[0:1], m[1:2], m[0:1]).astype(BF16)
    proj = _dot(h, w_ref[...])
    ql, kvl = DSA_Q_LORA, DSA_KV_LORA
    c_q = _rms(proj[:, :ql], qg_ref[...]).astype(BF16)
    ckv_ref[0] = _rms(proj[:, ql:ql + kvl], kvg_ref[...]).astype(BF16)
    q_ref[0] = _dot(c_q, wuq_ref[...]).astype(BF16)
    qi_ref[0] = _dot(c_q, wqi_ref[...]).astype(BF16)
    tail = proj[:, ql + kvl:ql + kvl + LANES]
    lane = lax.broadcasted_iota(jnp.int32, tail.shape, 1)
    is_key = lane < IDX_DIM
    mu = jnp.sum(jnp.where(is_key, tail, 0.0), axis=-1, keepdims=True) * (1.0 / IDX_DIM)
    dev = jnp.where(is_key, tail - mu, 0.0)
    var = jnp.sum(dev * dev, axis=-1, keepdims=True) * (1.0 / IDX_DIM)
    kn = dev * lax.rsqrt(var + NORM_EPS) * lng_ref[...] + lnb_ref[...]
    wscale = (IDX_HEADS ** -0.5) * (IDX_DIM ** -0.5)
    aux_ref[0] = jnp.where(is_key, kn, tail * wscale)


def _dsa_in(x, mod, ng, w, qg, kvg, lng, lnb, wuq, wqi, tm):
    b, s, d = x.shape
    outs = [
        (wuq.shape[1], BF16),
        (wqi.shape[1], BF16),
        (DSA_KV_LORA, BF16),
        (LANES, F32),
    ]
    return pl.pallas_call(
        _dsa_in_kernel,
        grid=(b, s // tm),
        in_specs=[
            pl.BlockSpec((1, tm, d), lambda i, j: (i, j, 0)),
            pl.BlockSpec((1, 6, d), lambda i, j: (i, 0, 0)),
            _const_spec(ng.shape),
            _const_spec(w.shape),
            _const_spec(qg.shape),
            _const_spec(kvg.shape),
            _const_spec(lng.shape),
            _const_spec(lnb.shape),
            _const_spec(wuq.shape),
            _const_spec(wqi.shape),
        ],
        out_specs=[pl.BlockSpec((1, tm, n), lambda i, j: (i, j, 0)) for n, _ in outs],
        out_shape=[jax.ShapeDtypeStruct((b, s, n), dt) for n, dt in outs],
        compiler_params=_cparams("parallel", "parallel"),
        name="dsa_in",
    )(x, mod, ng, w, qg, kvg, lng, lnb, wuq, wqi)


SEL_KC = 512


def _select_kernel(qi_ref, auxq_ref, auxk_ref, mask_ref, key_scr, jcut_scr, *, topk):
    i = pl.program_id(1)
    tq = qi_ref.shape[1]
    s = auxk_ref.shape[1]
    kc = min(SEL_KC, s)
    nkc = (i * tq + tq + kc - 1) // kc
    q = qi_ref[0]
    w = auxq_ref[0][:, IDX_DIM:IDX_DIM + IDX_HEADS]
    t = i * tq + lax.broadcasted_iota(jnp.int32, (tq, 1), 0)
    lane_pos = lax.broadcasted_iota(jnp.int32, (1, kc), 1)

    def score_body(j, carry):
        c0 = pl.multiple_of(j * kc, kc)
        kx = auxk_ref[0, pl.ds(c0, kc), :][:, :IDX_DIM].astype(BF16)
        sc = jnp.zeros((tq, kc), F32)
        for h in range(IDX_HEADS):
            lg = _dot_nt(q[:, h * IDX_DIM:(h + 1) * IDX_DIM], kx)
            sc = sc + w[:, h:h + 1] * jnp.maximum(lg, 0.0)
        bits = lax.bitcast_convert_type(sc, jnp.int32)
        key = bits ^ (lax.shift_right_arithmetic(bits, 31) & 0x7FFFFFFF)
        key = jnp.where(sc == 0.0, 0, key)
        key = jnp.where(c0 + lane_pos <= t, key, INT_MIN)
        key_scr[:, pl.ds(c0, kc)] = key
        return carry

    lax.fori_loop(0, nkc, score_body, 0)

    def count(hits):
        def body(j, acc):
            c0 = pl.multiple_of(j * kc, kc)
            hit = hits(key_scr[:, pl.ds(c0, kc)], c0 + lane_pos)
            for g in range(kc // LANES):
                acc = acc + hit[:, g * LANES:(g + 1) * LANES]
            return acc
        acc = lax.fori_loop(0, nkc, body, jnp.zeros((tq, LANES), F32))
        return jnp.sum(acc, axis=-1, keepdims=True)

    kf = float(topk)
    tau = jnp.where(count(lambda key, pos: jnp.where(key >= 0, 1.0, 0.0)) >= kf, 0, INT_MIN).astype(jnp.int32)
    tau = jnp.broadcast_to(tau, (tq, 1))

    def bit_body(b, tau):
        cand = tau + lax.shift_left(jnp.int32(1), 30 - b)
        return jnp.where(count(lambda key, pos: jnp.where(key >= cand, 1.0, 0.0)) >= kf, cand, tau)

    tau = lax.fori_loop(0, 31, bit_body, tau)

    need = kf - count(lambda key, pos: jnp.where(key > tau, 1.0, 0.0))
    n_ge = count(lambda key, pos: jnp.where(key >= tau, 1.0, 0.0))
    jcut_scr[...] = jnp.full(jcut_scr.shape, s, jnp.int32)

    @pl.when(jnp.max(n_ge) > kf)
    def _():
        nbits = max(1, (s - 1).bit_length())

        def pos_body(b, lo):
            cand = lo + lax.shift_left(jnp.int32(1), nbits - 1 - b)
            below = count(lambda key, pos: jnp.where(key == tau, jnp.where(pos < cand, 1.0, 0.0), 0.0))
            return jnp.where(below < need, cand, lo)

        jcut_scr[...] = lax.fori_loop(0, nbits, pos_body, jnp.zeros((tq, 1), jnp.int32))

    jcut = jcut_scr[...]

    def mask_body(j, carry):
        c0 = pl.multiple_of(j * kc, kc)
        key = key_scr[:, pl.ds(c0, kc)]
        pos = c0 + lane_pos
        sel = jnp.where(key > tau, 1, jnp.where(key == tau, jnp.where(pos <= jcut, 1, 0), 0))
        sel = jnp.where(pos <= t, sel, 0)
        mask_ref[0, :, pl.ds(c0, kc)] = sel.astype(jnp.int8)
        return carry

    lax.fori_loop(0, nkc, mask_body, 0)

    def zero_body(j, carry):
        c0 = pl.multiple_of(j * kc, kc)
        mask_ref[0, :, pl.ds(c0, kc)] = jnp.zeros((tq, kc), jnp.int8)
        return carry

    lax.fori_loop(nkc, s // kc, zero_body, 0)


def _dsa_select(qi, aux, topk):
    b, s, _ = qi.shape
    tq = Q_BLOCK
    return pl.pallas_call(
        functools.partial(_select_kernel, topk=topk),
        grid=(b, s // tq),
        in_specs=[
            pl.BlockSpec((1, tq, qi.shape[-1]), lambda i, j: (i, j, 0)),
            pl.BlockSpec((1, tq, LANES), lambda i, j: (i, j, 0)),
            pl.BlockSpec((1, s, LANES), lambda i, j: (i, 0, 0)),
        ],
        out_specs=pl.BlockSpec((1, tq, s), lambda i, j: (i, j, 0)),
        out_shape=jax.ShapeDtypeStruct((b, s, s), jnp.int8),
        scratch_shapes=[pltpu.VMEM((tq, s), jnp.int32), pltpu.VMEM((tq, 1), jnp.int32)],
        compiler_params=_cparams("parallel", "parallel"),
        name="dsa_select",
    )(qi, aux, aux)


ATT_KB = 512


def _attn_kernel(q_ref, ckv_ref, mask_ref, wuk_ref, wuv_ref, o_ref,
                 qlat_scr, m_scr, l_scr, acc_scr):
    i = pl.program_id(1)
    tq = q_ref.shape[1]
    s = ckv_ref.shape[1]
    nh, dqk, dl = wuk_ref.shape
    dv = wuv_ref.shape[2]
    kb = min(ATT_KB, s)
    nkb = (i * tq + tq + kb - 1) // kb

    q = q_ref[0]
    scale = dqk ** -0.5
    for h in range(nh):
        ql = _dot(q[:, h * dqk:(h + 1) * dqk], wuk_ref[h]) * scale
        qlat_scr[h * tq:(h + 1) * tq, :] = ql.astype(BF16)
    m_scr[...] = jnp.full(m_scr.shape, NEG_BIG, F32)
    l_scr[...] = jnp.zeros_like(l_scr)
    acc_scr[...] = jnp.zeros_like(acc_scr)

    def body(j, carry):
        c0 = pl.multiple_of(j * kb, kb)
        kv = ckv_ref[0, pl.ds(c0, kb), :]
        sc = _dot_nt(qlat_scr[...], kv)
        bias = (mask_ref[0, :, pl.ds(c0, kb)].astype(F32) - 1.0) * (-NEG_BIG)
        sc = (sc.reshape(nh, tq, kb) + bias[None]).reshape(nh * tq, kb)
        m_old = m_scr[...]
        m_new = jnp.maximum(m_old, jnp.max(sc, axis=-1, keepdims=True))
        alpha = jnp.exp(m_old - m_new)
        p = jnp.exp(sc - m_new)
        l_scr[...] = alpha * l_scr[...] + jnp.sum(p, axis=-1, keepdims=True)
        acc_scr[...] = alpha * acc_scr[...] + _dot(p.astype(BF16), kv)
        m_scr[...] = m_new
        return carry

    lax.fori_loop(0, nkb, body, 0)
    o_lat = (acc_scr[...] / l_scr[...]).astype(BF16)
    for h in range(nh):
        o_ref[0, :, h * dv:(h + 1) * dv] = _dot(o_lat[h * tq:(h + 1) * tq], wuv_ref[h]).astype(BF16)


def _dsa_attn(q, ckv, mask, wuk, wuv):
    b, s, _ = q.shape
    tq = Q_BLOCK
    nh, dqk, dl = wuk.shape
    dv = wuv.shape[2]
    return pl.pallas_call(
        _attn_kernel,
        grid=(b, s // tq),
        in_specs=[
            pl.BlockSpec((1, tq, q.shape[-1]), lambda i, j: (i, j, 0)),
            pl.BlockSpec((1, s, dl), lambda i, j: (i, 0, 0)),
            pl.BlockSpec((1, tq, s), lambda i, j: (i, j, 0)),
            _const_spec(wuk.shape),
            _const_spec(wuv.shape),
        ],
        out_specs=pl.BlockSpec((1, tq, nh * dv), lambda i, j: (i, j, 0)),
        out_shape=jax.ShapeDtypeStruct((b, s, nh * dv), BF16),
        scratch_shapes=[
            pltpu.VMEM((nh * tq, dl), BF16),
            pltpu.VMEM((nh * tq, 1), F32),
            pltpu.VMEM((nh * tq, 1), F32),
            pltpu.VMEM((nh * tq, dl), F32),
        ],
        compiler_params=_cparams("parallel", "parallel"),
        name="dsa_attn",
    )(q, ckv, mask, wuk, wuv)


def _gdn_layer(x, mod, ng, w_in, conv_w, a_log, dt_bias, norm_g, w_out, tm):
    b, s, _ = x.shape
    nmain = 2 * GDN_HEADS * GDN_DK + 2 * GDN_HEADS * GDN_DV
    proj, gates = _gdn_in(
        x, mod, ng, w_in[:, :nmain].astype(BF16), w_in[:, nmain:].astype(BF16),
        a_log.reshape(1, -1), dt_bias.reshape(1, -1), tm)
    grow = jnp.swapaxes(gates.reshape(b, s // GDN_CHUNK, GDN_CHUNK, -1), 2, 3)
    o = _gdn_core(proj, gates, grow, conv_w, norm_g.reshape(1, -1), min(tm, 256))
    return _outproj(o, x, mod, ng, w_out.astype(BF16), tm)


def _dsa_layer(x, mod, ng, w_in, qg, kvg, wuq, wuk, wuv, wqi, lng, lnb, w_out, tm):
    b, s, d = x.shape
    topk = min(TOPK_MAX, s // 4)
    n_in = w_in.shape[1]
    n_pad = DSA_Q_LORA + DSA_KV_LORA + LANES
    w_in_p = jnp.pad(w_in, ((0, 0), (0, n_pad - n_in))).astype(BF16)
    pad = LANES - IDX_DIM
    q, qi, ckv, aux = _dsa_in(
        x, mod, ng, w_in_p, qg.reshape(1, -1), kvg.reshape(1, -1),
        jnp.pad(lng, (0, pad)).reshape(1, -1), jnp.pad(lnb, (0, pad)).reshape(1, -1),
        wuq.astype(BF16), wqi.astype(BF16), tm)
    mask = _dsa_select(qi, aux, topk)
    o = _dsa_attn(q, ckv, mask, wuk.astype(BF16), wuv.astype(BF16))
    return _outproj(o, x, mod, ng, w_out.astype(BF16), tm)


def kernel(x, c, ada_w, ada_b, norm_g, ffn_w_in, ffn_w_out, gdn_w_in, gdn_conv_w, gdn_a_log,
           gdn_dt_bias, gdn_norm_g, gdn_w_out, dsa_w_in, dsa_q_norm_g, dsa_kv_norm_g, dsa_w_uq,
           dsa_w_uk, dsa_w_uv, dsa_w_qidx, dsa_kidx_ln_g, dsa_kidx_ln_b, dsa_w_out):
    b, s, d = x.shape
    depth = ada_w.shape[0]
    tm = min(512, s)
    fh = ffn_w_out.shape[1]
    fc = 256
    mods = _adaln_mod(c, ada_w, ada_b).reshape(depth, b, 6, d)
    for layer in range(depth):
        mod = mods[layer]
        ng = norm_g[layer]
        j = layer // 2
        if layer % 2 == 0:
            x = _gdn_layer(x, mod, ng, gdn_w_in[j], gdn_conv_w[j], gdn_a_log[j], gdn_dt_bias[j],
                           gdn_norm_g[j], gdn_w_out[j], tm)
        else:
            x = _dsa_layer(x, mod, ng, dsa_w_in[j], dsa_q_norm_g[j], dsa_kv_norm_g[j],
                           dsa_w_uq[j], dsa_w_uk[j], dsa_w_uv[j], dsa_w_qidx[j],
                           dsa_kidx_ln_g[j], dsa_kidx_ln_b[j], dsa_w_out[j], tm)
        w_in = ffn_w_in[layer].astype(BF16)
        wg = jnp.swapaxes(w_in[:, :fh].reshape(d, fh // fc, fc), 0, 1)
        wu = jnp.swapaxes(w_in[:, fh:].reshape(d, fh // fc, fc), 0, 1)
        wo = ffn_w_out[layer].astype(BF16).reshape(fh // fc, fc, d)
        x = _ffn(x, mod, ng, wg, wu, wo, tm)
    return x
```

```python
import functools

import jax
import jax.numpy as jnp
from jax import lax
from jax.experimental import pallas as pl
from jax.experimental.pallas import tpu as pltpu

NORM_EPS = 1e-6
L2_EPS = 1e-6

GDN_HEADS = 8
GDN_DK = 128
GDN_DV = 128
GDN_CONV = 4
GDN_CHUNK = 64

DSA_HEADS = 16
DSA_QK_DIM = 64
DSA_V_DIM = 64
DSA_Q_LORA = 256
DSA_KV_LORA = 256
IDX_HEADS = 8
IDX_DIM = 64
TOPK_MAX = 256
Q_BLOCK = 128

LANES = 128
VMEM_LIMIT = 56 * 1024 * 1024
NEG_BIG = -1e30
INT_MIN = -(2 ** 31)

F32 = jnp.float32
BF16 = jnp.bfloat16
HIGHEST = lax.Precision.HIGHEST


def _cparams(*sem):
    return pltpu.CompilerParams(dimension_semantics=sem, vmem_limit_bytes=VMEM_LIMIT)


def _dot(a, b, precision=None):
    return jnp.dot(a, b, preferred_element_type=F32, precision=precision)


def _dot_nt(a, b, precision=None):
    return lax.dot_general(a, b, (((1,), (1,)), ((), ())),
                           preferred_element_type=F32, precision=precision)


def _dot_tn(a, b, precision=None):
    return lax.dot_general(a, b, (((0,), (0,)), ((), ())),
                           preferred_element_type=F32, precision=precision)


def _split_bf16(a):
    hi = a.astype(BF16)
    return hi, (a - hi.astype(F32)).astype(BF16)


def _dot_split(a_hi, a_lo, b_hi, b_lo):
    return _dot(a_hi, b_hi) + (_dot(a_hi, b_lo) + _dot(a_lo, b_hi))


def _silu(x):
    return x * jax.nn.sigmoid(x)


def _rms(x, g):
    return x * lax.rsqrt(jnp.mean(x * x, axis=-1, keepdims=True) + NORM_EPS) * g


def _const_spec(shape):
    zeros = (0,) * len(shape)
    return pl.BlockSpec(shape, lambda *_: zeros)


def _mod_kernel(c_ref, w_ref, b_ref, o_ref):
    c = c_ref[...]
    o_ref[0] = _dot(_silu(c), w_ref[0], precision=HIGHEST) + b_ref[0]


def _adaln_mod(c, ada_w, ada_b):
    depth, d, n = ada_w.shape
    b = c.shape[0]
    tn = n // 4
    return pl.pallas_call(
        _mod_kernel,
        grid=(depth, n // tn),
        in_specs=[
            pl.BlockSpec((b, d), lambda l, j: (0, 0)),
            pl.BlockSpec((1, d, tn), lambda l, j: (l, 0, j)),
            pl.BlockSpec((1, 1, tn), lambda l, j: (l, 0, j)),
        ],
        out_specs=pl.BlockSpec((1, b, tn), lambda l, j: (l, 0, j)),
        out_shape=jax.ShapeDtypeStruct((depth, b, n), F32),
        compiler_params=_cparams("parallel", "parallel"),
        name="adaln_mod",
    )(c, ada_w, ada_b.reshape(depth, 1, n))


def _norm_mod(x, g, scale, shift):
    return _rms(x, g) * (1.0 + scale) + shift


def _ffn_kernel(x_ref, mod_ref, ng_ref, wg_ref, wu_ref, wo_ref, o_ref, h_scr, acc_scr):
    x = x_ref[0]
    m = mod_ref[0]
    ng = ng_ref[...]
    h_scr[...] = _norm_mod(x, ng[2:3], m[4:5], m[3:4]).astype(BF16)
    acc_scr[...] = jnp.zeros_like(acc_scr)

    def body(j, carry):
        h = h_scr[...]
        gate = _dot(h, wg_ref[j])
        up = _dot(h, wu_ref[j])
        a = (_silu(gate) * up).astype(BF16)
        acc_scr[...] += _dot(a, wo_ref[j])
        return carry

    lax.fori_loop(0, wg_ref.shape[0], body, 0)
    o_ref[0] = x + (1.0 + m[5:6]) * _rms(acc_scr[...], ng[3:4])


def _ffn(x, mod, ng, wg, wu, wo, tm):
    b, s, d = x.shape
    return pl.pallas_call(
        _ffn_kernel,
        grid=(b, s // tm),
        in_specs=[
            pl.BlockSpec((1, tm, d), lambda i, j: (i, j, 0)),
            pl.BlockSpec((1, 6, d), lambda i, j: (i, 0, 0)),
            _const_spec(ng.shape),
            _const_spec(wg.shape),
            _const_spec(wu.shape),
            _const_spec(wo.shape),
        ],
        out_specs=pl.BlockSpec((1, tm, d), lambda i, j: (i, j, 0)),
        out_shape=jax.ShapeDtypeStruct(x.shape, F32),
        scratch_shapes=[pltpu.VMEM((tm, d), BF16), pltpu.VMEM((tm, d), F32)],
        compiler_params=_cparams("parallel", "parallel"),
        name="ffn",
    )(x, mod, ng, wg, wu, wo)


def _outproj_kernel(o_ref, x_ref, mod_ref, ng_ref, w_ref, out_ref):
    m = mod_ref[0]
    y = _dot(o_ref[0], w_ref[...])
    out_ref[0] = x_ref[0] + (1.0 + m[2:3]) * _rms(y, ng_ref[1:2])


def _outproj(o, x, mod, ng, w, tm):
    b, s, d = x.shape
    return pl.pallas_call(
        _outproj_kernel,
        grid=(b, s // tm),
        in_specs=[
            pl.BlockSpec((1, tm, o.shape[-1]), lambda i, j: (i, j, 0)),
            pl.BlockSpec((1, tm, d), lambda i, j: (i, j, 0)),
            pl.BlockSpec((1, 6, d), lambda i, j: (i, 0, 0)),
            _const_spec(ng.shape),
            _const_spec(w.shape),
        ],
        out_specs=pl.BlockSpec((1, tm, d), lambda i, j: (i, j, 0)),
        out_shape=jax.ShapeDtypeStruct(x.shape, F32),
        compiler_params=_cparams("parallel", "parallel"),
        name="outproj",
    )(o, x, mod, ng, w)


def _gdn_in_kernel(x_ref, mod_ref, ng_ref, w_ref, wgate_ref, alog_ref, dtb_ref,
                   proj_ref, gates_ref, h_scr):
    x = x_ref[0]
    m = mod_ref[0]
    tm = x.shape[0]
    h_scr[...] = _norm_mod(x, ng_ref[0:1], m[1:2], m[0:1]).astype(BF16)
    h = h_scr[...]
    n = w_ref.shape[1]
    tn = 512
    for j in range(n // tn):
        proj_ref[0, :, j * tn:(j + 1) * tn] = _dot(h, w_ref[:, j * tn:(j + 1) * tn]).astype(BF16)
    raw = _dot(h, wgate_ref[...])
    nh = raw.shape[1] // 2
    beta = jax.nn.sigmoid(raw[:, :nh])
    a = raw[:, nh:] + dtb_ref[...]
    softplus = jnp.maximum(a, 0.0) + jnp.log1p(jnp.exp(-jnp.abs(a)))
    g = -jnp.exp(alog_ref[...]) * softplus
    r = lax.broadcasted_iota(jnp.int32, (tm, tm), 0)
    c = lax.broadcasted_iota(jnp.int32, (tm, tm), 1)
    shift = GDN_CHUNK.bit_length() - 1
    same_chunk = jnp.right_shift(r, shift) == jnp.right_shift(c, shift)
    mcum = jnp.where(same_chunk, jnp.where(r >= c, 1.0, 0.0), 0.0)
    gcum = _dot(mcum, g, precision=HIGHEST)
    gates_ref[0, :, 0:nh] = beta
    gates_ref[0, :, nh:2 * nh] = gcum


def _gdn_in(x, mod, ng, w, wgate, a_log, dt_bias, tm):
    b, s, d = x.shape
    n = w.shape[1]
    nh2 = wgate.shape[1]
    return pl.pallas_call(
        _gdn_in_kernel,
        grid=(b, s // tm),
        in_specs=[
            pl.BlockSpec((1, tm, d), lambda i, j: (i, j, 0)),
            pl.BlockSpec((1, 6, d), lambda i, j: (i, 0, 0)),
            _const_spec(ng.shape),
            _const_spec(w.shape),
            _const_spec(wgate.shape),
            _const_spec(a_log.shape),
            _const_spec(dt_bias.shape),
        ],
        out_specs=[
            pl.BlockSpec((1, tm, n), lambda i, j: (i, j, 0)),
            pl.BlockSpec((1, tm, nh2), lambda i, j: (i, j, 0)),
        ],
        out_shape=[
            jax.ShapeDtypeStruct((b, s, n), BF16),
            jax.ShapeDtypeStruct((b, s, nh2), F32),
        ],
        scratch_shapes=[pltpu.VMEM((tm, d), BF16)],
        compiler_params=_cparams("parallel", "parallel"),
        name="gdn_in",
    )(x, mod, ng, w, wgate, a_log, dt_bias)


def _gdn_kernel(proj_ref, gcol_ref, grow_ref, convw_ref, ng_ref, out_ref,
                conv_scr, carry_scr, state_scr):
    tb = proj_ref.shape[1]
    nh = GDN_HEADS
    dk = GDN_DK
    cs = GDN_CHUNK
    hk = nh * dk
    nqkv = conv_scr.shape[1]

    @pl.when(pl.program_id(1) == 0)
    def _():
        carry_scr[...] = jnp.zeros_like(carry_scr)
        state_scr[...] = jnp.zeros_like(state_scr)

    cw = 512
    for j in range(nqkv // cw):
        cols = slice(j * cw, (j + 1) * cw)
        raw = proj_ref[0, :, cols].astype(F32)
        ext = jnp.concatenate([carry_scr[:, cols], raw], axis=0)
        w = convw_ref[:, cols]
        y = w[0:1] * ext[5:5 + tb]
        for tap in range(1, GDN_CONV):
            y = y + w[tap:tap + 1] * ext[5 + tap:5 + tap + tb]
        conv_scr[:, cols] = _silu(y)
        carry_scr[:, cols] = raw[tb - 8:tb]

    ri = lax.broadcasted_iota(jnp.int32, (cs, cs), 0)
    ci = lax.broadcasted_iota(jnp.int32, (cs, cs), 1)
    lower = ri >= ci
    strict = ri > ci
    ng = ng_ref[...]

    heads = range(nh)
    n_rounds = (cs - 1).bit_length()

    def chunk_body(c, carry):
        r0 = pl.multiple_of(c * cs, cs)
        rows = pl.ds(r0, cs)
        gc = gcol_ref[0, rows, :]
        gr = grow_ref[0, c]
        qs, ks, rs, decays, qgs, kgs, glasts = [], [], [], [], [], [], []
        for h in heads:
            q = conv_scr[rows, h * dk:(h + 1) * dk]
            k = conv_scr[rows, hk + h * dk:hk + (h + 1) * dk]
            v = conv_scr[rows, 2 * hk + h * dk:2 * hk + (h + 1) * dk]
            q = q * (lax.rsqrt(jnp.sum(q * q, axis=-1, keepdims=True) + L2_EPS) * (dk ** -0.5))
            k = k * lax.rsqrt(jnp.sum(k * k, axis=-1, keepdims=True) + L2_EPS)
            beta = gc[:, h:h + 1]
            gcum = gc[:, nh + h:nh + h + 1]
            grow = gr[nh + h:nh + h + 1, :]
            glast = gcum[cs - 1:cs, :]
            eg = jnp.exp(gcum)
            kb = k * beta
            qs.append(jnp.concatenate([kb, q], axis=0).astype(BF16))
            ks.append(k.astype(BF16))
            rs.append(jnp.concatenate([v * beta, kb * eg], axis=1))
            decays.append(jnp.where(lower, jnp.exp(jnp.minimum(gcum - grow, 0.0)), 0.0))
            qgs.append((q * eg).astype(BF16))
            kgs.append((k * jnp.exp(glast - gcum)).astype(BF16))
            glasts.append(glast)
        ps, attns = [], []
        for h in heads:
            x = _dot_nt(qs[h], ks[h])
            ps.append(-jnp.where(strict, x[:cs] * decays[h], 0.0))
            attns.append((x[cs:] * decays[h]).astype(BF16))
        for j in range(n_rounds):
            for h in heads:
                p_hi, p_lo = _split_bf16(ps[h])
                rs[h] = rs[h] + _dot_split(p_hi, p_lo, *_split_bf16(rs[h]))
                if j + 1 < n_rounds:
                    ps[h] = _dot_split(p_hi, p_lo, p_hi, p_lo)
        v_news, sts = [], []
        for h in heads:
            st = state_scr[h]
            x = _dot(jnp.concatenate([rs[h][:, GDN_DV:].astype(BF16), qgs[h]], axis=0),
                     st.astype(BF16))
            v_news.append((rs[h][:, :GDN_DV] - x[:cs]).astype(BF16))
            qgs[h] = x[cs:]
            sts.append(st)
        for h in heads:
            o = qgs[h] + _dot(attns[h], v_news[h])
            state_scr[h] = sts[h] * jnp.exp(glasts[h]) + _dot_tn(kgs[h], v_news[h])
            z = proj_ref[0, rows, 3 * hk + h * dk:3 * hk + (h + 1) * dk].astype(F32)
            out_ref[0, rows, h * dk:(h + 1) * dk] = (_rms(o, ng) * _silu(z)).astype(BF16)
        return carry

    lax.fori_loop(0, tb // cs, chunk_body, 0)


def _gdn_core(proj, gates, grow, conv_w, norm_g, tb):
    b, s, n = proj.shape
    hv = GDN_HEADS * GDN_DV
    nqkv = conv_w.shape[1]
    return pl.pallas_call(
        _gdn_kernel,
        grid=(b, s // tb),
        in_specs=[
            pl.BlockSpec((1, tb, n), lambda i, j: (i, j, 0)),
            pl.BlockSpec((1, tb, gates.shape[-1]), lambda i, j: (i, j, 0)),
            pl.BlockSpec((1, tb // GDN_CHUNK) + grow.shape[2:], lambda i, j: (i, j, 0, 0)),
            _const_spec(conv_w.shape),
            _const_spec(norm_g.shape),
        ],
        out_specs=pl.BlockSpec((1, tb, hv), lambda i, j: (i, j, 0)),
        out_shape=jax.ShapeDtypeStruct((b, s, hv), BF16),
        scratch_shapes=[
            pltpu.VMEM((tb, nqkv), F32),
            pltpu.VMEM((8, nqkv), F32),
            pltpu.VMEM((GDN_HEADS, GDN_DK, GDN_DV), F32),
        ],
        compiler_params=_cparams("parallel", "arbitrary"),
        name="gdn_core",
    )(proj, gates, grow, conv_w, norm_g)


def _dsa_in_kernel(x_ref, mod_ref, ng_ref, w_ref, qg_ref, kvg_ref, lng_ref, lnb_ref,
                   wuq_ref, wqi_ref, q_ref, qi_ref, ckv_ref, aux_ref):
    x = x_ref[0]
    m = mod_ref[0]
    h = _norm_mod(x, ng_ref[0:1], m[1:2], m[0:1]).astype(BF16)
    proj = _dot(h, w_ref[...])
    ql, kvl = DSA_Q_LORA, DSA_KV_LORA
    c_q = _rms(proj[:, :ql], qg_ref[...]).astype(BF16)
    ckv_ref[0] = _rms(proj[:, ql:ql + kvl], kvg_ref[...]).astype(BF16)
    q_ref[0] = _dot(c_q, wuq_ref[...]).astype(BF16)
    qi_ref[0] = _dot(c_q, wqi_ref[...]).astype(BF16)
    tail = proj[:, ql + kvl:ql + kvl + LANES]
    lane = lax.broadcasted_iota(jnp.int32, tail.shape, 1)
    is_key = lane < IDX_DIM
    mu = jnp.sum(jnp.where(is_key, tail, 0.0), axis=-1, keepdims=True) * (1.0 / IDX_DIM)
    dev = jnp.where(is_key, tail - mu, 0.0)
    var = jnp.sum(dev * dev, axis=-1, keepdims=True) * (1.0 / IDX_DIM)
    kn = dev * lax.rsqrt(var + NORM_EPS) * lng_ref[...] + lnb_ref[...]
    wscale = (IDX_HEADS ** -0.5) * (IDX_DIM ** -0.5)
    aux_ref[0] = jnp.where(is_key, kn, tail * wscale)


def _dsa_in(x, mod, ng, w, qg, kvg, lng, lnb, wuq, wqi, tm):
    b, s, d = x.shape
    outs = [
        (wuq.shape[1], BF16),
        (wqi.shape[1], BF16),
        (DSA_KV_LORA, BF16),
        (LANES, F32),
    ]
    return pl.pallas_call(
        _dsa_in_kernel,
        grid=(b, s // tm),
        in_specs=[
            pl.BlockSpec((1, tm, d), lambda i, j: (i, j, 0)),
            pl.BlockSpec((1, 6, d), lambda i, j: (i, 0, 0)),
            _const_spec(ng.shape),
            _const_spec(w.shape),
            _const_spec(qg.shape),
            _const_spec(kvg.shape),
            _const_spec(lng.shape),
            _const_spec(lnb.shape),
            _const_spec(wuq.shape),
            _const_spec(wqi.shape),
        ],
        out_specs=[pl.BlockSpec((1, tm, n), lambda i, j: (i, j, 0)) for n, _ in outs],
        out_shape=[jax.ShapeDtypeStruct((b, s, n), dt) for n, dt in outs],
        compiler_params=_cparams("parallel", "parallel"),
        name="dsa_in",
    )(x, mod, ng, w, qg, kvg, lng, lnb, wuq, wqi)


SEL_KC = 512
COUNT_ROWS = 32


def _select_kernel(qi_ref, auxq_ref, auxk_ref, mask_ref, key_scr, jcut_scr, *, topk):
    i = pl.program_id(1)
    tq = qi_ref.shape[1]
    s = auxk_ref.shape[1]
    kc = min(SEL_KC, s)
    nkc = (i * tq + tq + kc - 1) // kc
    q = qi_ref[0]
    qstack = jnp.concatenate([q[:, h * IDX_DIM:(h + 1) * IDX_DIM] for h in range(IDX_HEADS)], axis=0)
    wt = auxq_ref[0].T[IDX_DIM:IDX_DIM + IDX_HEADS, :]
    t = i * tq + lax.broadcasted_iota(jnp.int32, (1, tq), 1)
    sub_pos = lax.broadcasted_iota(jnp.int32, (kc, 1), 0)

    def score_body(j, carry):
        c0 = pl.multiple_of(j * kc, kc)
        kx = auxk_ref[0, pl.ds(c0, kc), :][:, :IDX_DIM].astype(BF16)
        lg = _dot_nt(kx, qstack)
        sc = jnp.zeros((kc, tq), F32)
        for h in range(IDX_HEADS):
            sc = sc + wt[h:h + 1, :] * jnp.maximum(lg[:, h * tq:(h + 1) * tq], 0.0)
        bits = lax.bitcast_convert_type(sc, jnp.int32)
        key = bits ^ (lax.shift_right_arithmetic(bits, 31) & 0x7FFFFFFF)
        key = jnp.where(sc == 0.0, 0, key)
        key = jnp.where(c0 + sub_pos <= t, key, INT_MIN)
        key_scr[pl.ds(c0, kc), :] = key
        return carry

    lax.fori_loop(0, nkc, score_body, 0)

    def count(hits):
        def body(j, acc):
            c0 = pl.multiple_of(j * kc, kc)
            hit = hits(key_scr[pl.ds(c0, kc), :], c0 + sub_pos)
            return acc + jnp.sum(hit.reshape(kc // COUNT_ROWS, COUNT_ROWS, tq), axis=0)
        acc = lax.fori_loop(0, nkc, body, jnp.zeros((COUNT_ROWS, tq), F32))
        return jnp.sum(acc, axis=0, keepdims=True)

    kf = float(topk)
    tau = jnp.where(count(lambda key, pos: jnp.where(key >= 0, 1.0, 0.0)) >= kf, 0, INT_MIN).astype(jnp.int32)

    def bit_body(b, tau):
        cand = tau + lax.shift_left(jnp.int32(1), 30 - b)
        return jnp.where(count(lambda key, pos: jnp.where(key >= cand, 1.0, 0.0)) >= kf, cand, tau)

    tau = lax.fori_loop(0, 31, bit_body, tau)

    need = kf - count(lambda key, pos: jnp.where(key > tau, 1.0, 0.0))
    n_ge = count(lambda key, pos: jnp.where(key >= tau, 1.0, 0.0))
    jcut_scr[...] = jnp.full(jcut_scr.shape, s, jnp.int32)

    @pl.when(jnp.max(n_ge) > kf)
    def _():
        nbits = max(1, (s - 1).bit_length())

        def pos_body(b, lo):
            cand = lo + lax.shift_left(jnp.int32(1), nbits - 1 - b)
            below = count(lambda key, pos: jnp.where(key == tau, jnp.where(pos < cand, 1.0, 0.0), 0.0))
            return jnp.where(below < need, cand, lo)

        jcut_scr[...] = lax.fori_loop(0, nbits, pos_body, jnp.zeros((1, tq), jnp.int32))

    jcut = jcut_scr[...]

    def mask_body(j, carry):
        c0 = pl.multiple_of(j * kc, kc)
        key = key_scr[pl.ds(c0, kc), :]
        pos = c0 + sub_pos
        sel = jnp.where(key > tau, 1.0, jnp.where(key == tau, jnp.where(pos <= jcut, 1.0, 0.0), 0.0))
        sel = jnp.where(pos <= t, sel, 0.0)
        for g in range(kc // tq):
            tile = sel[g * tq:(g + 1) * tq, :].T
            mask_ref[0, :, pl.ds(pl.multiple_of(c0 + g * tq, tq), tq)] = tile.astype(jnp.int8)
        return carry

    lax.fori_loop(0, nkc, mask_body, 0)

    def zero_body(j, carry):
        c0 = pl.multiple_of(j * kc, kc)
        mask_ref[0, :, pl.ds(c0, kc)] = jnp.zeros((tq, kc), jnp.int8)
        return carry

    lax.fori_loop(nkc, s // kc, zero_body, 0)


def _dsa_select(qi, aux, topk):
    b, s, _ = qi.shape
    tq = Q_BLOCK
    return pl.pallas_call(
        functools.partial(_select_kernel, topk=topk),
        grid=(b, s // tq),
        in_specs=[
            pl.BlockSpec((1, tq, qi.shape[-1]), lambda i, j: (i, j, 0)),
            pl.BlockSpec((1, tq, LANES), lambda i, j: (i, j, 0)),
            pl.BlockSpec((1, s, LANES), lambda i, j: (i, 0, 0)),
        ],
        out_specs=pl.BlockSpec((1, tq, s), lambda i, j: (i, j, 0)),
        out_shape=jax.ShapeDtypeStruct((b, s, s), jnp.int8),
        scratch_shapes=[pltpu.VMEM((s, tq), jnp.int32), pltpu.VMEM((1, tq), jnp.int32)],
        compiler_params=_cparams("parallel", "parallel"),
        name="dsa_select",
    )(qi, aux, aux)


ATT_KB = 512
ATT_ROWS = 512
LOG2E = 1.4426950408889634


def _attn_kernel(q_ref, ckv_ref, mask_ref, wuk_ref, wuv_ref, o_ref,
                 qlat_scr, m_scr, l_scr, acc_scr):
    i = pl.program_id(1)
    tq = q_ref.shape[1]
    s = ckv_ref.shape[1]
    nh, dqk, dl = wuk_ref.shape
    dv = wuv_ref.shape[2]
    kb = min(ATT_KB, s)
    nkb = (i * tq + tq + kb - 1) // kb
    rg = ATT_ROWS
    hg = rg // tq

    q = q_ref[0]
    scale = (dqk ** -0.5) * LOG2E
    for h in range(nh):
        ql = _dot(q[:, h * dqk:(h + 1) * dqk], wuk_ref[h]) * scale
        qlat_scr[h * tq:(h + 1) * tq, :] = ql.astype(BF16)
    m_scr[...] = jnp.full(m_scr.shape, NEG_BIG, F32)
    l_scr[...] = jnp.zeros_like(l_scr)
    acc_scr[...] = jnp.zeros_like(acc_scr)
    n_groups = nh * tq // rg
    lane_tiles = kb // LANES

    def body(j, carry):
        c0 = pl.multiple_of(j * kb, kb)
        kv = ckv_ref[0, pl.ds(c0, kb), :]
        bias = (mask_ref[0, :, pl.ds(c0, kb)].astype(F32) - 1.0) * (-NEG_BIG)

        def scores(g):
            return _dot_nt(qlat_scr[g * rg:(g + 1) * rg, :], kv)

        sc_next = scores(0)
        for g in range(n_groups):
            rows = slice(g * rg, (g + 1) * rg)
            sc = sc_next
            if g + 1 < n_groups:
                sc_next = scores(g + 1)
            sc = (sc.reshape(hg, tq, kb) + bias[None]).reshape(rg, kb)
            part = sc[:, :LANES]
            for t in range(1, lane_tiles):
                part = jnp.maximum(part, sc[:, t * LANES:(t + 1) * LANES])
            m_old = m_scr[rows, :]
            m_new = jnp.maximum(m_old, jnp.max(part, axis=-1, keepdims=True))
            alpha = jnp.exp2(m_old - m_new)
            p = jnp.exp2(sc - jnp.tile(m_new, (1, lane_tiles)))
            psum = p[:, :LANES]
            for t in range(1, lane_tiles):
                psum = psum + p[:, t * LANES:(t + 1) * LANES]
            l_scr[rows, :] = alpha * l_scr[rows, :] + psum
            acc_scr[rows, :] = (jnp.tile(alpha, (1, dl // LANES)) * acc_scr[rows, :]
                                + _dot(p.astype(BF16), kv))
            m_scr[rows, :] = m_new
        return carry

    lax.fori_loop(0, nkb, body, 0)
    l_tot = jnp.sum(l_scr[...], axis=-1, keepdims=True)
    o_lat = (acc_scr[...] / l_tot).astype(BF16)
    for h in range(nh):
        o_ref[0, :, h * dv:(h + 1) * dv] = _dot(o_lat[h * tq:(h + 1) * tq], wuv_ref[h]).astype(BF16)


def _dsa_attn(q, ckv, mask, wuk, wuv):
    b, s, _ = q.shape
    tq = Q_BLOCK
    nh, dqk, dl = wuk.shape
    dv = wuv.shape[2]
    return pl.pallas_call(
        _attn_kernel,
        grid=(b, s // tq),
        in_specs=[
            pl.BlockSpec((1, tq, q.shape[-1]), lambda i, j: (i, j, 0)),
            pl.BlockSpec((1, s, dl), lambda i, j: (i, 0, 0)),
            pl.BlockSpec((1, tq, s), lambda i, j: (i, j, 0)),
            _const_spec(wuk.shape),
            _const_spec(wuv.shape),
        ],
        out_specs=pl.BlockSpec((1, tq, nh * dv), lambda i, j: (i, j, 0)),
        out_shape=jax.ShapeDtypeStruct((b, s, nh * dv), BF16),
        scratch_shapes=[
            pltpu.VMEM((nh * tq, dl), BF16),
            pltpu.VMEM((nh * tq, LANES), F32),
            pltpu.VMEM((nh * tq, LANES), F32),
            pltpu.VMEM((nh * tq, dl), F32),
        ],
        compiler_params=_cparams("parallel", "parallel"),
        name="dsa_attn",
    )(q, ckv, mask, wuk, wuv)


def _gdn_layer(x, mod, ng, w_in, conv_w, a_log, dt_bias, norm_g, w_out, tm):
    b, s, _ = x.shape
    nmain = 2 * GDN_HEADS * GDN_DK + 2 * GDN_HEADS * GDN_DV
    proj, gates = _gdn_in(
        x, mod, ng, w_in[:, :nmain].astype(BF16), w_in[:, nmain:].astype(BF16),
        a_log.reshape(1, -1), dt_bias.reshape(1, -1), tm)
    grow = jnp.swapaxes(gates.reshape(b, s // GDN_CHUNK, GDN_CHUNK, -1), 2, 3)
    o = _gdn_core(proj, gates, grow, conv_w, norm_g.reshape(1, -1), min(tm, 256))
    return _outproj(o, x, mod, ng, w_out.astype(BF16), tm)


def _dsa_layer(x, mod, ng, w_in, qg, kvg, wuq, wuk, wuv, wqi, lng, lnb, w_out, tm):
    b, s, d = x.shape
    topk = min(TOPK_MAX, s // 4)
    n_in = w_in.shape[1]
    n_pad = DSA_Q_LORA + DSA_KV_LORA + LANES
    w_in_p = jnp.pad(w_in, ((0, 0), (0, n_pad - n_in))).astype(BF16)
    pad = LANES - IDX_DIM
    q, qi, ckv, aux = _dsa_in(
        x, mod, ng, w_in_p, qg.reshape(1, -1), kvg.reshape(1, -1),
        jnp.pad(lng, (0, pad)).reshape(1, -1), jnp.pad(lnb, (0, pad)).reshape(1, -1),
        wuq.astype(BF16), wqi.astype(BF16), tm)
    mask = _dsa_select(qi, aux, topk)
    o = _dsa_attn(q, ckv, mask, wuk.astype(BF16), wuv.astype(BF16))
    return _outproj(o, x, mod, ng, w_out.astype(BF16), tm)


def kernel(x, c, ada_w, ada_b, norm_g, ffn_w_in, ffn_w_out, gdn_w_in, gdn_conv_w, gdn_a_log,
           gdn_dt_bias, gdn_norm_g, gdn_w_out, dsa_w_in, dsa_q_norm_g, dsa_kv_norm_g, dsa_w_uq,
           dsa_w_uk, dsa_w_uv, dsa_w_qidx, dsa_kidx_ln_g, dsa_kidx_ln_b, dsa_w_out):
    b, s, d = x.shape
    depth = ada_w.shape[0]
    tm = min(512, s)
    fh = ffn_w_out.shape[1]
    fc = 256
    mods = _adaln_mod(c, ada_w, ada_b).reshape(depth, b, 6, d)
    for layer in range(depth):
        mod = mods[layer]
        ng = norm_g[layer]
        j = layer // 2
        if layer % 2 == 0:
            x = _gdn_layer(x, mod, ng, gdn_w_in[j], gdn_conv_w[j], gdn_a_log[j], gdn_dt_bias[j],
                           gdn_norm_g[j], gdn_w_out[j], tm)
        else:
            x = _dsa_layer(x, mod, ng, dsa_w_in[j], dsa_q_norm_g[j], dsa_kv_norm_g[j],
                           dsa_w_uq[j], dsa_w_uk[j], dsa_w_uv[j], dsa_w_qidx[j],
                           dsa_kidx_ln_g[j], dsa_kidx_ln_b[j], dsa_w_out[j], tm)
        w_in = ffn_w_in[layer].astype(BF16)
        wg = jnp.swapaxes(w_in[:, :fh].reshape(d, fh // fc, fc), 0, 1)
        wu = jnp.swapaxes(w_in[:, fh:].reshape(d, fh // fc, fc), 0, 1)
        wo = ffn_w_out[layer].astype(BF16).reshape(fh // fc, fc, d)
        x = _ffn(x, mod, ng, wg, wu, wo, tm)
    return x
```

```python
import functools

import jax
import jax.numpy as jnp
from jax import lax
from jax.experimental import pallas as pl
from jax.experimental.pallas import tpu as pltpu

NORM_EPS = 1e-6
L2_EPS = 1e-6

GDN_HEADS = 8
GDN_DK = 128
GDN_DV = 128
GDN_CONV = 4
GDN_CHUNK = 64

DSA_HEADS = 16
DSA_QK_DIM = 64
DSA_V_DIM = 64
DSA_Q_LORA = 256
DSA_KV_LORA = 256
IDX_HEADS = 8
IDX_DIM = 64
TOPK_MAX = 256
Q_BLOCK = 128

LANES = 128
VMEM_LIMIT = 56 * 1024 * 1024
NEG_BIG = -1e30
INT_MIN = -(2 ** 31)

F32 = jnp.float32
BF16 = jnp.bfloat16
HIGHEST = lax.Precision.HIGHEST


def _cparams(*sem):
    return pltpu.CompilerParams(dimension_semantics=sem, vmem_limit_bytes=VMEM_LIMIT)


def _dot(a, b, precision=None):
    return jnp.dot(a, b, preferred_element_type=F32, precision=precision)


def _dot_nt(a, b, precision=None):
    return lax.dot_general(a, b, (((1,), (1,)), ((), ())),
                           preferred_element_type=F32, precision=precision)


def _dot_tn(a, b, precision=None):
    return lax.dot_general(a, b, (((0,), (0,)), ((), ())),
                           preferred_element_type=F32, precision=precision)


def _bdot(a, b):
    return _dot(a.astype(BF16), b.astype(BF16))


def _silu(x):
    return x * jax.nn.sigmoid(x)


def _rms(x, g):
    return x * lax.rsqrt(jnp.mean(x * x, axis=-1, keepdims=True) + NORM_EPS) * g


def _const_spec(shape):
    zeros = (0,) * len(shape)
    return pl.BlockSpec(shape, lambda *_: zeros)


def _mod_kernel(c_ref, w_ref, b_ref, o_ref):
    c = c_ref[...]
    o_ref[0] = _dot(_silu(c), w_ref[0], precision=HIGHEST) + b_ref[0]


def _adaln_mod(c, ada_w, ada_b):
    depth, d, n = ada_w.shape
    b = c.shape[0]
    tn = n // 4
    return pl.pallas_call(
        _mod_kernel,
        grid=(depth, n // tn),
        in_specs=[
            pl.BlockSpec((b, d), lambda l, j: (0, 0)),
            pl.BlockSpec((1, d, tn), lambda l, j: (l, 0, j)),
            pl.BlockSpec((1, 1, tn), lambda l, j: (l, 0, j)),
        ],
        out_specs=pl.BlockSpec((1, b, tn), lambda l, j: (l, 0, j)),
        out_shape=jax.ShapeDtypeStruct((depth, b, n), F32),
        compiler_params=_cparams("parallel", "parallel"),
        name="adaln_mod",
    )(c, ada_w, ada_b.reshape(depth, 1, n))


def _norm_mod(x, g, scale, shift):
    return _rms(x, g) * (1.0 + scale) + shift


def _ffn_kernel(x_ref, mod_ref, ng_ref, wg_ref, wu_ref, wo_ref, o_ref, h_scr, acc_scr):
    x = x_ref[0]
    m = mod_ref[0]
    ng = ng_ref[...]
    h_scr[...] = _norm_mod(x, ng[2:3], m[4:5], m[3:4]).astype(BF16)
    acc_scr[...] = jnp.zeros_like(acc_scr)

    def body(j, carry):
        h = h_scr[...]
        gate = _dot(h, wg_ref[j])
        up = _dot(h, wu_ref[j])
        a = (_silu(gate) * up).astype(BF16)
        acc_scr[...] += _dot(a, wo_ref[j])
        return carry

    lax.fori_loop(0, wg_ref.shape[0], body, 0)
    o_ref[0] = x + (1.0 + m[5:6]) * _rms(acc_scr[...], ng[3:4])


def _ffn(x, mod, ng, wg, wu, wo, tm):
    b, s, d = x.shape
    return pl.pallas_call(
        _ffn_kernel,
        grid=(b, s // tm),
        in_specs=[
            pl.BlockSpec((1, tm, d), lambda i, j: (i, j, 0)),
            pl.BlockSpec((1, 6, d), lambda i, j: (i, 0, 0)),
            _const_spec(ng.shape),
            _const_spec(wg.shape),
            _const_spec(wu.shape),
            _const_spec(wo.shape),
        ],
        out_specs=pl.BlockSpec((1, tm, d), lambda i, j: (i, j, 0)),
        out_shape=jax.ShapeDtypeStruct(x.shape, F32),
        scratch_shapes=[pltpu.VMEM((tm, d), BF16), pltpu.VMEM((tm, d), F32)],
        compiler_params=_cparams("parallel", "parallel"),
        name="ffn",
    )(x, mod, ng, wg, wu, wo)


def _outproj_kernel(o_ref, x_ref, mod_ref, ng_ref, w_ref, out_ref):
    m = mod_ref[0]
    y = _dot(o_ref[0], w_ref[...])
    out_ref[0] = x_ref[0] + (1.0 + m[2:3]) * _rms(y, ng_ref[1:2])


def _outproj(o, x, mod, ng, w, tm):
    b, s, d = x.shape
    return pl.pallas_call(
        _outproj_kernel,
        grid=(b, s // tm),
        in_specs=[
            pl.BlockSpec((1, tm, o.shape[-1]), lambda i, j: (i, j, 0)),
            pl.BlockSpec((1, tm, d), lambda i, j: (i, j, 0)),
            pl.BlockSpec((1, 6, d), lambda i, j: (i, 0, 0)),
            _const_spec(ng.shape),
            _const_spec(w.shape),
        ],
        out_specs=pl.BlockSpec((1, tm, d), lambda i, j: (i, j, 0)),
        out_shape=jax.ShapeDtypeStruct(x.shape, F32),
        compiler_params=_cparams("parallel", "parallel"),
        name="outproj",
    )(o, x, mod, ng, w)


def _gdn_in_kernel(x_ref, mod_ref, ng_ref, w_ref, wgate_ref, alog_ref, dtb_ref,
                   proj_ref, gates_ref, h_scr):
    x = x_ref[0]
    m = mod_ref[0]
    tm = x.shape[0]
    h_scr[...] = _norm_mod(x, ng_ref[0:1], m[1:2], m[0:1]).astype(BF16)
    h = h_scr[...]
    n = w_ref.shape[1]
    tn = 512
    for j in range(n // tn):
        proj_ref[0, :, j * tn:(j + 1) * tn] = _dot(h, w_ref[:, j * tn:(j + 1) * tn]).astype(BF16)
    raw = _dot(h, wgate_ref[...])
    nh = raw.shape[1] // 2
    beta = jax.nn.sigmoid(raw[:, :nh])
    a = raw[:, nh:] + dtb_ref[...]
    softplus = jnp.maximum(a, 0.0) + jnp.log1p(jnp.exp(-jnp.abs(a)))
    g = -jnp.exp(alog_ref[...]) * softplus
    r = lax.broadcasted_iota(jnp.int32, (tm, tm), 0)
    c = lax.broadcasted_iota(jnp.int32, (tm, tm), 1)
    shift = GDN_CHUNK.bit_length() - 1
    same_chunk = jnp.right_shift(r, shift) == jnp.right_shift(c, shift)
    mcum = jnp.where(same_chunk, jnp.where(r >= c, 1.0, 0.0), 0.0)
    gcum = _dot(mcum, g, precision=HIGHEST)
    gates_ref[0, :, 0:nh] = beta
    gates_ref[0, :, nh:2 * nh] = gcum


def _gdn_in(x, mod, ng, w, wgate, a_log, dt_bias, tm):
    b, s, d = x.shape
    n = w.shape[1]
    nh2 = wgate.shape[1]
    return pl.pallas_call(
        _gdn_in_kernel,
        grid=(b, s // tm),
        in_specs=[
            pl.BlockSpec((1, tm, d), lambda i, j: (i, j, 0)),
            pl.BlockSpec((1, 6, d), lambda i, j: (i, 0, 0)),
            _const_spec(ng.shape),
            _const_spec(w.shape),
            _const_spec(wgate.shape),
            _const_spec(a_log.shape),
            _const_spec(dt_bias.shape),
        ],
        out_specs=[
            pl.BlockSpec((1, tm, n), lambda i, j: (i, j, 0)),
            pl.BlockSpec((1, tm, nh2), lambda i, j: (i, j, 0)),
        ],
        out_shape=[
            jax.ShapeDtypeStruct((b, s, n), BF16),
            jax.ShapeDtypeStruct((b, s, nh2), F32),
        ],
        scratch_shapes=[pltpu.VMEM((tm, d), BF16)],
        compiler_params=_cparams("parallel", "parallel"),
        name="gdn_in",
    )(x, mod, ng, w, wgate, a_log, dt_bias)


SOLVE_BASE = 8
GDN_CHUNKS_PER_STEP = 4


def _gdn_kernel(proj_ref, gcol_ref, grow_ref, convw_ref, ng_ref, out_ref,
                conv_scr, carry_scr, state_scr):
    tb = proj_ref.shape[1]
    nh = GDN_HEADS
    dk = GDN_DK
    cs = GDN_CHUNK
    hk = nh * dk
    nqkv = conv_scr.shape[1]

    @pl.when(pl.program_id(1) == 0)
    def _():
        carry_scr[...] = jnp.zeros_like(carry_scr)
        state_scr[...] = jnp.zeros_like(state_scr)

    cw = 512
    for j in range(nqkv // cw):
        cols = slice(j * cw, (j + 1) * cw)
        raw = proj_ref[0, :, cols].astype(F32)
        ext = jnp.concatenate([carry_scr[:, cols], raw], axis=0)
        w = convw_ref[:, cols]
        y = w[0:1] * ext[5:5 + tb]
        for tap in range(1, GDN_CONV):
            y = y + w[tap:tap + 1] * ext[5 + tap:5 + tap + tb]
        conv_scr[:, cols] = _silu(y)
        carry_scr[:, cols] = raw[tb - 8:tb]

    ri = lax.broadcasted_iota(jnp.int32, (cs, cs), 0)
    ci = lax.broadcasted_iota(jnp.int32, (cs, cs), 1)
    lower = ri >= ci
    strict = ri > ci
    ng = ng_ref[...]

    heads = range(nh)
    eye = jnp.where(ri == ci, 1.0, 0.0)

    def same_block(width):
        shift = width.bit_length() - 1
        return jnp.right_shift(ri, shift) == jnp.right_shift(ci, shift)

    base_blocks = same_block(SOLVE_BASE)
    merge_masks = []
    width = SOLVE_BASE
    while width < cs:
        merge_masks.append((same_block(width), same_block(2 * width)))
        width *= 2

    nc = GDN_CHUNKS_PER_STEP
    units = [(cc, h) for cc in range(nc) for h in heads]
    every = range(len(units))

    def chunk_body(c, carry):
        qs, ks, rs, decays, qgs, kgs, glasts, rowss = [], [], [], [], [], [], [], []
        for cc, h in units:
            r0 = pl.multiple_of((c * nc + cc) * cs, cs)
            rows = pl.ds(r0, cs)
            gc = gcol_ref[0, rows, :]
            gr = grow_ref[0, c * nc + cc]
            rowss.append(rows)
            q = conv_scr[rows, h * dk:(h + 1) * dk]
            k = conv_scr[rows, hk + h * dk:hk + (h + 1) * dk]
            v = conv_scr[rows, 2 * hk + h * dk:2 * hk + (h + 1) * dk]
            q = q * (lax.rsqrt(jnp.sum(q * q, axis=-1, keepdims=True) + L2_EPS) * (dk ** -0.5))
            k = k * lax.rsqrt(jnp.sum(k * k, axis=-1, keepdims=True) + L2_EPS)
            beta = gc[:, h:h + 1]
            gcum = gc[:, nh + h:nh + h + 1]
            grow = gr[nh + h:nh + h + 1, :]
            glast = gcum[cs - 1:cs, :]
            eg = jnp.exp(gcum)
            kb = k * beta
            qs.append(jnp.concatenate([kb, q], axis=0).astype(BF16))
            ks.append(k.astype(BF16))
            rs.append(jnp.concatenate([v * beta, kb * eg], axis=1))
            decays.append(jnp.where(lower, jnp.exp(jnp.minimum(gcum - grow, 0.0)), 0.0))
            qgs.append((q * eg).astype(BF16))
            kgs.append((k * jnp.exp(glast - gcum)).astype(BF16))
            glasts.append(glast)
        lows, attns = [], []
        for u in every:
            x = _dot_nt(qs[u], ks[u])
            lows.append(jnp.where(strict, x[:cs] * decays[u], 0.0))
            attns.append((x[cs:] * decays[u]).astype(BF16))
        ts, pws = [], []
        for u in every:
            m = -jnp.where(base_blocks, lows[u], 0.0)
            ts.append(eye + m)
            pws.append(m)
        for j in range(1, (SOLVE_BASE - 1).bit_length()):
            for u in every:
                pws[u] = _bdot(pws[u], pws[u])
                ts[u] = ts[u] + _bdot(ts[u], pws[u])
        for inner, outer in merge_masks:
            for u in every:
                tb16 = ts[u].astype(BF16)
                coupling = jnp.where(outer, jnp.where(inner, 0.0, lows[u]), 0.0)
                ts[u] = ts[u] - _bdot(tb16, _dot(coupling.astype(BF16), tb16))
        for u in every:
            rs[u] = _bdot(ts[u], rs[u])
        for cc in range(nc):
            v_news, sts = [], []
            for h in heads:
                u = cc * nh + h
                st = state_scr[h]
                x = _dot(jnp.concatenate([rs[u][:, GDN_DV:].astype(BF16), qgs[u]], axis=0),
                         st.astype(BF16))
                v_news.append((rs[u][:, :GDN_DV] - x[:cs]).astype(BF16))
                qgs[u] = x[cs:]
                sts.append(st)
            for h in heads:
                u = cc * nh + h
                o = qgs[u] + _dot(attns[u], v_news[h])
                state_scr[h] = sts[h] * jnp.exp(glasts[u]) + _dot_tn(kgs[u], v_news[h])
                z = proj_ref[0, rowss[u], 3 * hk + h * dk:3 * hk + (h + 1) * dk].astype(F32)
                out_ref[0, rowss[u], h * dk:(h + 1) * dk] = (_rms(o, ng) * _silu(z)).astype(BF16)
        return carry

    lax.fori_loop(0, tb // (cs * nc), chunk_body, 0)


def _gdn_core(proj, gates, grow, conv_w, norm_g, tb):
    b, s, n = proj.shape
    hv = GDN_HEADS * GDN_DV
    nqkv = conv_w.shape[1]
    return pl.pallas_call(
        _gdn_kernel,
        grid=(b, s // tb),
        in_specs=[
            pl.BlockSpec((1, tb, n), lambda i, j: (i, j, 0)),
            pl.BlockSpec((1, tb, gates.shape[-1]), lambda i, j: (i, j, 0)),
            pl.BlockSpec((1, tb // GDN_CHUNK) + grow.shape[2:], lambda i, j: (i, j, 0, 0)),
            _const_spec(conv_w.shape),
            _const_spec(norm_g.shape),
        ],
        out_specs=pl.BlockSpec((1, tb, hv), lambda i, j: (i, j, 0)),
        out_shape=jax.ShapeDtypeStruct((b, s, hv), BF16),
        scratch_shapes=[
            pltpu.VMEM((tb, nqkv), F32),
            pltpu.VMEM((8, nqkv), F32),
            pltpu.VMEM((GDN_HEADS, GDN_DK, GDN_DV), F32),
        ],
        compiler_params=_cparams("parallel", "arbitrary"),
        name="gdn_core",
    )(proj, gates, grow, conv_w, norm_g)


def _dsa_in_kernel(x_ref, mod_ref, ng_ref, w_ref, qg_ref, kvg_ref, lng_ref, lnb_ref,
                   wuq_ref, wqi_ref, q_ref, qi_ref, ckv_ref, aux_ref):
    x = x_ref[0]
    m = mod_ref[0]
    h = _norm_mod(x, ng_ref[0:1], m[1:2], m[0:1]).astype(BF16)
    proj = _dot(h, w_ref[...])
    ql, kvl = DSA_Q_LORA, DSA_KV_LORA
    c_q = _rms(proj[:, :ql], qg_ref[...]).astype(BF16)
    ckv_ref[0] = _rms(proj[:, ql:ql + kvl], kvg_ref[...]).astype(BF16)
    q_ref[0] = _dot(c_q, wuq_ref[...]).astype(BF16)
    qi_ref[0] = _dot(c_q, wqi_ref[...]).astype(BF16)
    tail = proj[:, ql + kvl:ql + kvl + LANES]
    lane = lax.broadcasted_iota(jnp.int32, tail.shape, 1)
    is_key = lane < IDX_DIM
    mu = jnp.sum(jnp.where(is_key, tail, 0.0), axis=-1, keepdims=True) * (1.0 / IDX_DIM)
    dev = jnp.where(is_key, tail - mu, 0.0)
    var = jnp.sum(dev * dev, axis=-1, keepdims=True) * (1.0 / IDX_DIM)
    kn = dev * lax.rsqrt(var + NORM_EPS) * lng_ref[...] + lnb_ref[...]
    wscale = (IDX_HEADS ** -0.5) * (IDX_DIM ** -0.5)
    aux_ref[0] = jnp.where(is_key, kn, tail * wscale)


def _dsa_in(x, mod, ng, w, qg, kvg, lng, lnb, wuq, wqi, tm):
    b, s, d = x.shape
    outs = [
        (wuq.shape[1], BF16),
        (wqi.shape[1], BF16),
        (DSA_KV_LORA, BF16),
        (LANES, F32),
    ]
    return pl.pallas_call(
        _dsa_in_kernel,
        grid=(b, s // tm),
        in_specs=[
            pl.BlockSpec((1, tm, d), lambda i, j: (i, j, 0)),
            pl.BlockSpec((1, 6, d), lambda i, j: (i, 0, 0)),
            _const_spec(ng.shape),
            _const_spec(w.shape),
            _const_spec(qg.shape),
            _const_spec(kvg.shape),
            _const_spec(lng.shape),
            _const_spec(lnb.shape),
            _const_spec(wuq.shape),
            _const_spec(wqi.shape),
        ],
        out_specs=[pl.BlockSpec((1, tm, n), lambda i, j: (i, j, 0)) for n, _ in outs],
        out_shape=[jax.ShapeDtypeStruct((b, s, n), dt) for n, dt in outs],
        compiler_params=_cparams("parallel", "parallel"),
        name="dsa_in",
    )(x, mod, ng, w, qg, kvg, lng, lnb, wuq, wqi)


SEL_KC = 512
COUNT_ROWS = 32


def _select_kernel(qi_ref, auxq_ref, auxk_ref, mask_ref, key_scr, jcut_scr, *, topk):
    i = pl.program_id(1)
    tq = qi_ref.shape[1]
    s = auxk_ref.shape[1]
    kc = min(SEL_KC, s)
    nkc = (i * tq + tq + kc - 1) // kc
    q = qi_ref[0]
    qstack = jnp.concatenate([q[:, h * IDX_DIM:(h + 1) * IDX_DIM] for h in range(IDX_HEADS)], axis=0)
    wt = auxq_ref[0].T[IDX_DIM:IDX_DIM + IDX_HEADS, :]
    t = i * tq + lax.broadcasted_iota(jnp.int32, (1, tq), 1)
    sub_pos = lax.broadcasted_iota(jnp.int32, (kc, 1), 0)

    def score_body(j, carry):
        c0 = pl.multiple_of(j * kc, kc)
        kx = auxk_ref[0, pl.ds(c0, kc), :][:, :IDX_DIM].astype(BF16)
        lg = _dot_nt(kx, qstack)
        sc = jnp.zeros((kc, tq), F32)
        for h in range(IDX_HEADS):
            sc = sc + wt[h:h + 1, :] * jnp.maximum(lg[:, h * tq:(h + 1) * tq], 0.0)
        bits = lax.bitcast_convert_type(sc, jnp.int32)
        key = bits ^ (lax.shift_right_arithmetic(bits, 31) & 0x7FFFFFFF)
        key = jnp.where(sc == 0.0, 0, key)
        key = jnp.where(c0 + sub_pos <= t, key, INT_MIN)
        key_scr[pl.ds(c0, kc), :] = key
        return carry

    lax.fori_loop(0, nkc, score_body, 0)

    def count(hits):
        def body(j, acc):
            c0 = pl.multiple_of(j * kc, kc)
            hit = hits(key_scr[pl.ds(c0, kc), :], c0 + sub_pos)
            return acc + jnp.sum(hit.reshape(kc // COUNT_ROWS, COUNT_ROWS, tq), axis=0)
        acc = lax.fori_loop(0, nkc, body, jnp.zeros((COUNT_ROWS, tq), F32))
        return jnp.sum(acc, axis=0, keepdims=True)

    kf = float(topk)
    tau = jnp.where(count(lambda key, pos: jnp.where(key >= 0, 1.0, 0.0)) >= kf, 0, INT_MIN).astype(jnp.int32)

    def bit_body(b, tau):
        cand = tau + lax.shift_left(jnp.int32(1), 30 - b)
        return jnp.where(count(lambda key, pos: jnp.where(key >= cand, 1.0, 0.0)) >= kf, cand, tau)

    tau = lax.fori_loop(0, 31, bit_body, tau)

    need = kf - count(lambda key, pos: jnp.where(key > tau, 1.0, 0.0))
    n_ge = count(lambda key, pos: jnp.where(key >= tau, 1.0, 0.0))
    jcut_scr[...] = jnp.full(jcut_scr.shape, s, jnp.int32)

    @pl.when(jnp.max(n_ge) > kf)
    def _():
        nbits = max(1, (s - 1).bit_length())

        def pos_body(b, lo):
            cand = lo + lax.shift_left(jnp.int32(1), nbits - 1 - b)
            below = count(lambda key, pos: jnp.where(key == tau, jnp.where(pos < cand, 1.0, 0.0), 0.0))
            return jnp.where(below < need, cand, lo)

        jcut_scr[...] = lax.fori_loop(0, nbits, pos_body, jnp.zeros((1, tq), jnp.int32))

    jcut = jcut_scr[...]

    def mask_body(j, carry):
        c0 = pl.multiple_of(j * kc, kc)
        key = key_scr[pl.ds(c0, kc), :]
        pos = c0 + sub_pos
        sel = jnp.where(key > tau, 1.0, jnp.where(key == tau, jnp.where(pos <= jcut, 1.0, 0.0), 0.0))
        sel = jnp.where(pos <= t, sel, 0.0)
        for g in range(kc // tq):
            tile = sel[g * tq:(g + 1) * tq, :].T
            mask_ref[0, :, pl.ds(pl.multiple_of(c0 + g * tq, tq), tq)] = tile.astype(jnp.int8)
        return carry

    lax.fori_loop(0, nkc, mask_body, 0)

    def zero_body(j, carry):
        c0 = pl.multiple_of(j * kc, kc)
        mask_ref[0, :, pl.ds(c0, kc)] = jnp.zeros((tq, kc), jnp.int8)
        return carry

    lax.fori_loop(nkc, s // kc, zero_body, 0)


def _dsa_select(qi, aux, topk):
    b, s, _ = qi.shape
    tq = Q_BLOCK
    return pl.pallas_call(
        functools.partial(_select_kernel, topk=topk),
        grid=(b, s // tq),
        in_specs=[
            pl.BlockSpec((1, tq, qi.shape[-1]), lambda i, j: (i, j, 0)),
            pl.BlockSpec((1, tq, LANES), lambda i, j: (i, j, 0)),
            pl.BlockSpec((1, s, LANES), lambda i, j: (i, 0, 0)),
        ],
        out_specs=pl.BlockSpec((1, tq, s), lambda i, j: (i, j, 0)),
        out_shape=jax.ShapeDtypeStruct((b, s, s), jnp.int8),
        scratch_shapes=[pltpu.VMEM((s, tq), jnp.int32), pltpu.VMEM((1, tq), jnp.int32)],
        compiler_params=_cparams("parallel", "parallel"),
        name="dsa_select",
    )(qi, aux, aux)


ATT_KB = 512
ATT_ROWS = 512
LOG2E = 1.4426950408889634


def _attn_kernel(q_ref, ckv_ref, mask_ref, wuk_ref, wuv_ref, o_ref,
                 qlat_scr, m_scr, l_scr, acc_scr):
    i = pl.program_id(1)
    tq = q_ref.shape[1]
    s = ckv_ref.shape[1]
    nh, dqk, dl = wuk_ref.shape
    dv = wuv_ref.shape[2]
    kb = min(ATT_KB, s)
    nkb = (i * tq + tq + kb - 1) // kb
    rg = ATT_ROWS
    hg = rg // tq

    q = q_ref[0]
    scale = (dqk ** -0.5) * LOG2E
    for h in range(nh):
        ql = _dot(q[:, h * dqk:(h + 1) * dqk], wuk_ref[h]) * scale
        qlat_scr[h * tq:(h + 1) * tq, :] = ql.astype(BF16)
    m_scr[...] = jnp.full(m_scr.shape, NEG_BIG, F32)
    l_scr[...] = jnp.zeros_like(l_scr)
    acc_scr[...] = jnp.zeros_like(acc_scr)
    n_groups = nh * tq // rg
    lane_tiles = kb // LANES

    def body(j, carry):
        c0 = pl.multiple_of(j * kb, kb)
        kv = ckv_ref[0, pl.ds(c0, kb), :]
        bias = (mask_ref[0, :, pl.ds(c0, kb)].astype(F32) - 1.0) * (-NEG_BIG)

        def scores(g):
            return _dot_nt(qlat_scr[g * rg:(g + 1) * rg, :], kv)

        sc_next = scores(0)
        for g in range(n_groups):
            rows = slice(g * rg, (g + 1) * rg)
            sc = sc_next
            if g + 1 < n_groups:
                sc_next = scores(g + 1)
            sc = (sc.reshape(hg, tq, kb) + bias[None]).reshape(rg, kb)
            part = sc[:, :LANES]
            for t in range(1, lane_tiles):
                part = jnp.maximum(part, sc[:, t * LANES:(t + 1) * LANES])
            m_old = m_scr[rows, :]
            m_new = jnp.maximum(m_old, jnp.max(part, axis=-1, keepdims=True))
            alpha = jnp.exp2(m_old - m_new)
            p = jnp.exp2(sc - jnp.tile(m_new, (1, lane_tiles)))
            psum = p[:, :LANES]
            for t in range(1, lane_tiles):
                psum = psum + p[:, t * LANES:(t + 1) * LANES]
            l_scr[rows, :] = alpha * l_scr[rows, :] + psum
            acc_scr[rows, :] = (jnp.tile(alpha, (1, dl // LANES)) * acc_scr[rows, :]
                                + _dot(p.astype(BF16), kv))
            m_scr[rows, :] = m_new
        return carry

    lax.fori_loop(0, nkb, body, 0)
    l_tot = jnp.sum(l_scr[...], axis=-1, keepdims=True)
    o_lat = (acc_scr[...] / l_tot).astype(BF16)
    for h in range(nh):
        o_ref[0, :, h * dv:(h + 1) * dv] = _dot(o_lat[h * tq:(h + 1) * tq], wuv_ref[h]).astype(BF16)


def _dsa_attn(q, ckv, mask, wuk, wuv):
    b, s, _ = q.shape
    tq = Q_BLOCK
    nh, dqk, dl = wuk.shape
    dv = wuv.shape[2]
    return pl.pallas_call(
        _attn_kernel,
        grid=(b, s // tq),
        in_specs=[
            pl.BlockSpec((1, tq, q.shape[-1]), lambda i, j: (i, j, 0)),
            pl.BlockSpec((1, s, dl), lambda i, j: (i, 0, 0)),
            pl.BlockSpec((1, tq, s), lambda i, j: (i, j, 0)),
            _const_spec(wuk.shape),
            _const_spec(wuv.shape),
        ],
        out_specs=pl.BlockSpec((1, tq, nh * dv), lambda i, j: (i, j, 0)),
        out_shape=jax.ShapeDtypeStruct((b, s, nh * dv), BF16),
        scratch_shapes=[
            pltpu.VMEM((nh * tq, dl), BF16),
            pltpu.VMEM((nh * tq, LANES), F32),
            pltpu.VMEM((nh * tq, LANES), F32),
            pltpu.VMEM((nh * tq, dl), F32),
        ],
        compiler_params=_cparams("parallel", "parallel"),
        name="dsa_attn",
    )(q, ckv, mask, wuk, wuv)


def _gdn_layer(x, mod, ng, w_in, conv_w, a_log, dt_bias, norm_g, w_out, tm):
    b, s, _ = x.shape
    nmain = 2 * GDN_HEADS * GDN_DK + 2 * GDN_HEADS * GDN_DV
    proj, gates = _gdn_in(
        x, mod, ng, w_in[:, :nmain].astype(BF16), w_in[:, nmain:].astype(BF16),
        a_log.reshape(1, -1), dt_bias.reshape(1, -1), tm)
    grow = jnp.swapaxes(gates.reshape(b, s // GDN_CHUNK, GDN_CHUNK, -1), 2, 3)
    o = _gdn_core(proj, gates, grow, conv_w, norm_g.reshape(1, -1), min(tm, 256))
    return _outproj(o, x, mod, ng, w_out.astype(BF16), tm)


def _dsa_layer(x, mod, ng, w_in, qg, kvg, wuq, wuk, wuv, wqi, lng, lnb, w_out, tm):
    b, s, d = x.shape
    topk = min(TOPK_MAX, s // 4)
    n_in = w_in.shape[1]
    n_pad = DSA_Q_LORA + DSA_KV_LORA + LANES
    w_in_p = jnp.pad(w_in, ((0, 0), (0, n_pad - n_in))).astype(BF16)
    pad = LANES - IDX_DIM
    q, qi, ckv, aux = _dsa_in(
        x, mod, ng, w_in_p, qg.reshape(1, -1), kvg.reshape(1, -1),
        jnp.pad(lng, (0, pad)).reshape(1, -1), jnp.pad(lnb, (0, pad)).reshape(1, -1),
        wuq.astype(BF16), wqi.astype(BF16), tm)
    mask = _dsa_select(qi, aux, topk)
    o = _dsa_attn(q, ckv, mask, wuk.astype(BF16), wuv.astype(BF16))
    return _outproj(o, x, mod, ng, w_out.astype(BF16), tm)


def kernel(x, c, ada_w, ada_b, norm_g, ffn_w_in, ffn_w_out, gdn_w_in, gdn_conv_w, gdn_a_log,
           gdn_dt_bias, gdn_norm_g, gdn_w_out, dsa_w_in, dsa_q_norm_g, dsa_kv_norm_g, dsa_w_uq,
           dsa_w_uk, dsa_w_uv, dsa_w_qidx, dsa_kidx_ln_g, dsa_kidx_ln_b, dsa_w_out):
    b, s, d = x.shape
    depth = ada_w.shape[0]
    tm = min(512, s)
    fh = ffn_w_out.shape[1]
    fc = 256
    mods = _adaln_mod(c, ada_w, ada_b).reshape(depth, b, 6, d)
    for layer in range(depth):
        mod = mods[layer]
        ng = norm_g[layer]
        j = layer // 2
        if layer % 2 == 0:
            x = _gdn_layer(x, mod, ng, gdn_w_in[j], gdn_conv_w[j], gdn_a_log[j], gdn_dt_bias[j],
                           gdn_norm_g[j], gdn_w_out[j], tm)
        else:
            x = _dsa_layer(x, mod, ng, dsa_w_in[j], dsa_q_norm_g[j], dsa_kv_norm_g[j],
                           dsa_w_uq[j], dsa_w_uk[j], dsa_w_uv[j], dsa_w_qidx[j],
                           dsa_kidx_ln_g[j], dsa_kidx_ln_b[j], dsa_w_out[j], tm)
        w_in = ffn_w_in[layer].astype(BF16)
        wg = jnp.swapaxes(w_in[:, :fh].reshape(d, fh // fc, fc), 0, 1)
        wu = jnp.swapaxes(w_in[:, fh:].reshape(d, fh // fc, fc), 0, 1)
        wo = ffn_w_out[layer].astype(BF16).reshape(fh // fc, fc, d)
        x = _ffn(x, mod, ng, wg, wu, wo, tm)
    return x
```

```python
import functools

import jax
import jax.numpy as jnp
from jax import lax
from jax.experimental import pallas as pl
from jax.experimental.pallas import tpu as pltpu

NORM_EPS = 1e-6
L2_EPS = 1e-6

GDN_HEADS = 8
GDN_DK = 128
GDN_DV = 128
GDN_CONV = 4
GDN_CHUNK = 64

DSA_HEADS = 16
DSA_QK_DIM = 64
DSA_V_DIM = 64
DSA_Q_LORA = 256
DSA_KV_LORA = 256
IDX_HEADS = 8
IDX_DIM = 64
TOPK_MAX = 256
Q_BLOCK = 128

LANES = 128
VMEM_LIMIT = 56 * 1024 * 1024
NEG_BIG = -(2.0 ** 100)
INT_MIN = -(2 ** 31)

F32 = jnp.float32
BF16 = jnp.bfloat16
HIGHEST = lax.Precision.HIGHEST


def _cparams(*sem):
    return pltpu.CompilerParams(dimension_semantics=sem, vmem_limit_bytes=VMEM_LIMIT)


def _dot(a, b, precision=None):
    return jnp.dot(a, b, preferred_element_type=F32, precision=precision)


def _dot_nt(a, b, precision=None):
    return lax.dot_general(a, b, (((1,), (1,)), ((), ())),
                           preferred_element_type=F32, precision=precision)


def _dot_tn(a, b, precision=None):
    return lax.dot_general(a, b, (((0,), (0,)), ((), ())),
                           preferred_element_type=F32, precision=precision)


def _bdot(a, b):
    return _dot(a.astype(BF16), b.astype(BF16))


def _silu(x):
    return x * jax.nn.sigmoid(x)


def _rms(x, g):
    return x * lax.rsqrt(jnp.mean(x * x, axis=-1, keepdims=True) + NORM_EPS) * g


def _const_spec(shape):
    zeros = (0,) * len(shape)
    return pl.BlockSpec(shape, lambda *_: zeros)


def _mod_kernel(c_ref, w_ref, b_ref, o_ref):
    c = c_ref[...]
    o_ref[0] = _dot(_silu(c), w_ref[0], precision=HIGHEST) + b_ref[0]


def _adaln_mod(c, ada_w, ada_b):
    depth, d, n = ada_w.shape
    b = c.shape[0]
    tn = n // 4
    return pl.pallas_call(
        _mod_kernel,
        grid=(depth, n // tn),
        in_specs=[
            pl.BlockSpec((b, d), lambda l, j: (0, 0)),
            pl.BlockSpec((1, d, tn), lambda l, j: (l, 0, j)),
            pl.BlockSpec((1, 1, tn), lambda l, j: (l, 0, j)),
        ],
        out_specs=pl.BlockSpec((1, b, tn), lambda l, j: (l, 0, j)),
        out_shape=jax.ShapeDtypeStruct((depth, b, n), F32),
        compiler_params=_cparams("parallel", "parallel"),
        name="adaln_mod",
    )(c, ada_w, ada_b.reshape(depth, 1, n))


def _norm_mod(x, g, scale, shift):
    return _rms(x, g) * (1.0 + scale) + shift


def _ffn_kernel(x_ref, mod_ref, ng_ref, wg_ref, wu_ref, wo_ref, o_ref, h_scr, acc_scr):
    x = x_ref[0]
    m = mod_ref[0]
    ng = ng_ref[...]
    h_scr[...] = _norm_mod(x, ng[2:3], m[4:5], m[3:4]).astype(BF16)
    acc_scr[...] = jnp.zeros_like(acc_scr)

    def body(j, carry):
        h = h_scr[...]
        gate = _dot(h, wg_ref[j])
        up = _dot(h, wu_ref[j])
        a = (_silu(gate) * up).astype(BF16)
        acc_scr[...] += _dot(a, wo_ref[j])
        return carry

    lax.fori_loop(0, wg_ref.shape[0], body, 0, unroll=True)
    o_ref[0] = x + (1.0 + m[5:6]) * _rms(acc_scr[...], ng[3:4])


def _ffn(x, mod, ng, wg, wu, wo, tm):
    b, s, d = x.shape
    return pl.pallas_call(
        _ffn_kernel,
        grid=(b, s // tm),
        in_specs=[
            pl.BlockSpec((1, tm, d), lambda i, j: (i, j, 0)),
            pl.BlockSpec((1, 6, d), lambda i, j: (i, 0, 0)),
            _const_spec(ng.shape),
            _const_spec(wg.shape),
            _const_spec(wu.shape),
            _const_spec(wo.shape),
        ],
        out_specs=pl.BlockSpec((1, tm, d), lambda i, j: (i, j, 0)),
        out_shape=jax.ShapeDtypeStruct(x.shape, F32),
        scratch_shapes=[pltpu.VMEM((tm, d), BF16), pltpu.VMEM((tm, d), F32)],
        compiler_params=_cparams("parallel", "parallel"),
        name="ffn",
    )(x, mod, ng, wg, wu, wo)


def _outproj_kernel(o_ref, x_ref, mod_ref, ng_ref, w_ref, out_ref):
    m = mod_ref[0]
    y = _dot(o_ref[0], w_ref[...])
    out_ref[0] = x_ref[0] + (1.0 + m[2:3]) * _rms(y, ng_ref[1:2])


def _outproj(o, x, mod, ng, w, tm):
    b, s, d = x.shape
    return pl.pallas_call(
        _outproj_kernel,
        grid=(b, s // tm),
        in_specs=[
            pl.BlockSpec((1, tm, o.shape[-1]), lambda i, j: (i, j, 0)),
            pl.BlockSpec((1, tm, d), lambda i, j: (i, j, 0)),
            pl.BlockSpec((1, 6, d), lambda i, j: (i, 0, 0)),
            _const_spec(ng.shape),
            _const_spec(w.shape),
        ],
        out_specs=pl.BlockSpec((1, tm, d), lambda i, j: (i, j, 0)),
        out_shape=jax.ShapeDtypeStruct(x.shape, F32),
        compiler_params=_cparams("parallel", "parallel"),
        name="outproj",
    )(o, x, mod, ng, w)


def _gdn_in_kernel(x_ref, mod_ref, ng_ref, w_ref, wgate_ref, alog_ref, dtb_ref,
                   proj_ref, gates_ref, h_scr):
    x = x_ref[0]
    m = mod_ref[0]
    tm = x.shape[0]
    h_scr[...] = _norm_mod(x, ng_ref[0:1], m[1:2], m[0:1]).astype(BF16)
    h = h_scr[...]
    n = w_ref.shape[1]
    tn = 512
    for j in range(n // tn):
        proj_ref[0, :, j * tn:(j + 1) * tn] = _dot(h, w_ref[:, j * tn:(j + 1) * tn]).astype(BF16)
    raw = _dot(h, wgate_ref[...])
    nh = raw.shape[1] // 2
    beta = jax.nn.sigmoid(raw[:, :nh])
    a = raw[:, nh:] + dtb_ref[...]
    softplus = jnp.maximum(a, 0.0) + jnp.log1p(jnp.exp(-jnp.abs(a)))
    g = -jnp.exp(alog_ref[...]) * softplus
    r = lax.broadcasted_iota(jnp.int32, (tm, tm), 0)
    c = lax.broadcasted_iota(jnp.int32, (tm, tm), 1)
    shift = GDN_CHUNK.bit_length() - 1
    same_chunk = jnp.right_shift(r, shift) == jnp.right_shift(c, shift)
    mcum = jnp.where(same_chunk, jnp.where(r >= c, 1.0, 0.0), 0.0).astype(BF16)
    g_hi = g.astype(BF16)
    g_mid = (g - g_hi.astype(F32)).astype(BF16)
    g_lo = (g - g_hi.astype(F32) - g_mid.astype(F32)).astype(BF16)
    parts = _dot(mcum, jnp.concatenate([g_hi, g_mid, g_lo], axis=1))
    gcum = parts[:, :nh] + (parts[:, nh:2 * nh] + parts[:, 2 * nh:])
    gates_ref[0, :, 0:nh] = beta
    gates_ref[0, :, nh:2 * nh] = gcum


def _gdn_in(x, mod, ng, w, wgate, a_log, dt_bias, tm):
    b, s, d = x.shape
    n = w.shape[1]
    nh2 = wgate.shape[1]
    return pl.pallas_call(
        _gdn_in_kernel,
        grid=(b, s // tm),
        in_specs=[
            pl.BlockSpec((1, tm, d), lambda i, j: (i, j, 0)),
            pl.BlockSpec((1, 6, d), lambda i, j: (i, 0, 0)),
            _const_spec(ng.shape),
            _const_spec(w.shape),
            _const_spec(wgate.shape),
            _const_spec(a_log.shape),
            _const_spec(dt_bias.shape),
        ],
        out_specs=[
            pl.BlockSpec((1, tm, n), lambda i, j: (i, j, 0)),
            pl.BlockSpec((1, tm, nh2), lambda i, j: (i, j, 0)),
        ],
        out_shape=[
            jax.ShapeDtypeStruct((b, s, n), BF16),
            jax.ShapeDtypeStruct((b, s, nh2), F32),
        ],
        scratch_shapes=[pltpu.VMEM((tm, d), BF16)],
        compiler_params=_cparams("parallel", "parallel"),
        name="gdn_in",
    )(x, mod, ng, w, wgate, a_log, dt_bias)


SOLVE_BASE = 8
GDN_CHUNKS_PER_STEP = 4


def _gdn_kernel(proj_ref, gcol_ref, grow_ref, convw_ref, ng_ref, shift_ref, out_ref,
                conv_scr, carry_scr, state_scr):
    tb = proj_ref.shape[1]
    nh = GDN_HEADS
    dk = GDN_DK
    cs = GDN_CHUNK
    hk = nh * dk
    nqkv = conv_scr.shape[1]

    @pl.when(pl.program_id(1) == 0)
    def _():
        carry_scr[...] = jnp.zeros_like(carry_scr)
        state_scr[...] = jnp.zeros_like(state_scr)

    cw = 512
    row8 = lax.broadcasted_iota(jnp.int32, (8, cw), 0)
    for j in range(nqkv // cw):
        cols = slice(j * cw, (j + 1) * cw)
        u = proj_ref[0, :, cols]
        shifted = _dot(shift_ref[...], u)
        prev = carry_scr[:, cols]
        w = convw_ref[:, cols]
        raw = u.astype(F32)
        y = w[GDN_CONV - 1:GDN_CONV] * raw
        head = y[0:8]
        for d in range(1, GDN_CONV):
            wd = w[GDN_CONV - 1 - d:GDN_CONV - d]
            y = y + wd * shifted[(d - 1) * tb:d * tb]
            head = head + wd * (shifted[(d - 1) * tb:(d - 1) * tb + 8]
                                + jnp.where(row8 < d, pltpu.roll(prev, d, axis=0), 0.0))
        conv_scr[:, cols] = _silu(y)
        conv_scr[0:8, cols] = _silu(head)
        carry_scr[:, cols] = raw[tb - 8:tb]

    ri = lax.broadcasted_iota(jnp.int32, (cs, cs), 0)
    ci = lax.broadcasted_iota(jnp.int32, (cs, cs), 1)
    lower = ri >= ci
    strict = ri > ci
    ng = ng_ref[...]

    heads = range(nh)
    eye = jnp.where(ri == ci, 1.0, 0.0)

    def same_block(width):
        shift = width.bit_length() - 1
        return jnp.right_shift(ri, shift) == jnp.right_shift(ci, shift)

    base_blocks = same_block(SOLVE_BASE)
    merge_masks = []
    width = SOLVE_BASE
    while width < cs:
        merge_masks.append((same_block(width), same_block(2 * width)))
        width *= 2

    nc = GDN_CHUNKS_PER_STEP
    units = [(cc, h) for cc in range(nc) for h in heads]
    every = range(len(units))

    def chunk_body(c, carry):
        qs, ks, rs, decays, qgs, kgs, glasts, rowss = [], [], [], [], [], [], [], []
        for cc, h in units:
            r0 = pl.multiple_of((c * nc + cc) * cs, cs)
            rows = pl.ds(r0, cs)
            gc = gcol_ref[0, rows, :]
            gr = grow_ref[0, c * nc + cc]
            rowss.append(rows)
            q = conv_scr[rows, h * dk:(h + 1) * dk]
            k = conv_scr[rows, hk + h * dk:hk + (h + 1) * dk]
            v = conv_scr[rows, 2 * hk + h * dk:2 * hk + (h + 1) * dk]
            q = q * (lax.rsqrt(jnp.sum(q * q, axis=-1, keepdims=True) + L2_EPS) * (dk ** -0.5))
            k = k * lax.rsqrt(jnp.sum(k * k, axis=-1, keepdims=True) + L2_EPS)
            beta = gc[:, h:h + 1]
            gcum = gc[:, nh + h:nh + h + 1]
            grow = gr[nh + h:nh + h + 1, :]
            glast = gcum[cs - 1:cs, :]
            eg = jnp.exp(gcum)
            kb = k * beta
            qs.append(jnp.concatenate([kb, q], axis=0).astype(BF16))
            ks.append(k.astype(BF16))
            rs.append(jnp.concatenate([v * beta, kb * eg], axis=1))
            decays.append(jnp.where(lower, jnp.exp(jnp.minimum(gcum - grow, 0.0)), 0.0))
            qgs.append((q * eg).astype(BF16))
            kgs.append((k * jnp.exp(glast - gcum)).astype(BF16))
            glasts.append(glast)
        lows, attns = [], []
        for u in every:
            x = _dot_nt(qs[u], ks[u])
            lows.append(jnp.where(strict, x[:cs] * decays[u], 0.0))
            attns.append((x[cs:] * decays[u]).astype(BF16))
        ts, pws = [], []
        for u in every:
            m = -jnp.where(base_blocks, lows[u], 0.0)
            ts.append(eye + m)
            pws.append(m)
        for j in range(1, (SOLVE_BASE - 1).bit_length()):
            for u in every:
                pws[u] = _bdot(pws[u], pws[u])
                ts[u] = ts[u] + _bdot(ts[u], pws[u])
        for inner, outer in merge_masks:
            for u in every:
                tb16 = ts[u].astype(BF16)
                coupling = jnp.where(outer, jnp.where(inner, 0.0, lows[u]), 0.0)
                ts[u] = ts[u] - _bdot(tb16, _dot(coupling.astype(BF16), tb16))
        for u in every:
            rs[u] = _bdot(ts[u], rs[u])
        for cc in range(nc):
            v_news, sts = [], []
            for h in heads:
                u = cc * nh + h
                st = state_scr[h]
                x = _dot(jnp.concatenate([rs[u][:, GDN_DV:].astype(BF16), qgs[u]], axis=0),
                         st.astype(BF16))
                v_news.append((rs[u][:, :GDN_DV] - x[:cs]).astype(BF16))
                qgs[u] = x[cs:]
                sts.append(st)
            for h in heads:
                u = cc * nh + h
                o = qgs[u] + _dot(attns[u], v_news[h])
                state_scr[h] = sts[h] * jnp.exp(glasts[u]) + _dot_tn(kgs[u], v_news[h])
                z = proj_ref[0, rowss[u], 3 * hk + h * dk:3 * hk + (h + 1) * dk].astype(F32)
                out_ref[0, rowss[u], h * dk:(h + 1) * dk] = (_rms(o, ng) * _silu(z)).astype(BF16)
        return carry

    lax.fori_loop(0, tb // (cs * nc), chunk_body, 0)


def _gdn_core(proj, gates, grow, conv_w, norm_g, tb):
    b, s, n = proj.shape
    hv = GDN_HEADS * GDN_DV
    nqkv = conv_w.shape[1]
    t_out = jnp.arange(tb)[:, None]
    t_in = jnp.arange(tb)[None, :]
    shift = jnp.concatenate([(t_out - d == t_in) for d in range(1, GDN_CONV)], axis=0).astype(BF16)
    return pl.pallas_call(
        _gdn_kernel,
        grid=(b, s // tb),
        in_specs=[
            pl.BlockSpec((1, tb, n), lambda i, j: (i, j, 0)),
            pl.BlockSpec((1, tb, gates.shape[-1]), lambda i, j: (i, j, 0)),
            pl.BlockSpec((1, tb // GDN_CHUNK) + grow.shape[2:], lambda i, j: (i, j, 0, 0)),
            _const_spec(conv_w.shape),
            _const_spec(norm_g.shape),
            _const_spec(shift.shape),
        ],
        out_specs=pl.BlockSpec((1, tb, hv), lambda i, j: (i, j, 0)),
        out_shape=jax.ShapeDtypeStruct((b, s, hv), BF16),
        scratch_shapes=[
            pltpu.VMEM((tb, nqkv), F32),
            pltpu.VMEM((8, nqkv), F32),
            pltpu.VMEM((GDN_HEADS, GDN_DK, GDN_DV), F32),
        ],
        compiler_params=_cparams("parallel", "arbitrary"),
        name="gdn_core",
    )(proj, gates, grow, conv_w, norm_g, shift)


def _dsa_in_kernel(x_ref, mod_ref, ng_ref, w_ref, qg_ref, kvg_ref, lng_ref, lnb_ref,
                   wuq_ref, wqi_ref, q_ref, qi_ref, ckv_ref, aux_ref):
    x = x_ref[0]
    m = mod_ref[0]
    h = _norm_mod(x, ng_ref[0:1], m[1:2], m[0:1]).astype(BF16)
    proj = _dot(h, w_ref[...])
    ql, kvl = DSA_Q_LORA, DSA_KV_LORA
    c_q = _rms(proj[:, :ql], qg_ref[...]).astype(BF16)
    ckv_ref[0] = _rms(proj[:, ql:ql + kvl], kvg_ref[...]).astype(BF16)
    q_ref[0] = _dot(c_q, wuq_ref[...]).astype(BF16)
    qi_ref[0] = _dot(c_q, wqi_ref[...]).astype(BF16)
    tail = proj[:, ql + kvl:ql + kvl + LANES]
    lane = lax.broadcasted_iota(jnp.int32, tail.shape, 1)
    is_key = lane < IDX_DIM
    mu = jnp.sum(jnp.where(is_key, tail, 0.0), axis=-1, keepdims=True) * (1.0 / IDX_DIM)
    dev = jnp.where(is_key, tail - mu, 0.0)
    var = jnp.sum(dev * dev, axis=-1, keepdims=True) * (1.0 / IDX_DIM)
    kn = dev * lax.rsqrt(var + NORM_EPS) * lng_ref[...] + lnb_ref[...]
    wscale = (IDX_HEADS ** -0.5) * (IDX_DIM ** -0.5)
    aux_ref[0] = jnp.where(is_key, kn, tail * wscale)


def _dsa_in(x, mod, ng, w, qg, kvg, lng, lnb, wuq, wqi, tm):
    b, s, d = x.shape
    outs = [
        (wuq.shape[1], BF16),
        (wqi.shape[1], BF16),
        (DSA_KV_LORA, BF16),
        (LANES, F32),
    ]
    return pl.pallas_call(
        _dsa_in_kernel,
        grid=(b, s // tm),
        in_specs=[
            pl.BlockSpec((1, tm, d), lambda i, j: (i, j, 0)),
            pl.BlockSpec((1, 6, d), lambda i, j: (i, 0, 0)),
            _const_spec(ng.shape),
            _const_spec(w.shape),
            _const_spec(qg.shape),
            _const_spec(kvg.shape),
            _const_spec(lng.shape),
            _const_spec(lnb.shape),
            _const_spec(wuq.shape),
            _const_spec(wqi.shape),
        ],
        out_specs=[pl.BlockSpec((1, tm, n), lambda i, j: (i, j, 0)) for n, _ in outs],
        out_shape=[jax.ShapeDtypeStruct((b, s, n), dt) for n, dt in outs],
        compiler_params=_cparams("parallel", "parallel"),
        name="dsa_in",
    )(x, mod, ng, w, qg, kvg, lng, lnb, wuq, wqi)


SEL_KC = 512
COUNT_ROWS = 32


def _select_kernel(qi_ref, auxq_ref, auxk_ref, mask_ref, key_scr, jcut_scr, *, topk):
    i = pl.program_id(1)
    tq = qi_ref.shape[1]
    s = auxk_ref.shape[1]
    kc = min(SEL_KC, s)
    nkc = (i * tq + tq + kc - 1) // kc
    q = qi_ref[0]
    qstack = jnp.concatenate([q[:, h * IDX_DIM:(h + 1) * IDX_DIM] for h in range(IDX_HEADS)], axis=0)
    wt = auxq_ref[0].T[IDX_DIM:IDX_DIM + IDX_HEADS, :]
    t = i * tq + lax.broadcasted_iota(jnp.int32, (1, tq), 1)
    sub_pos = lax.broadcasted_iota(jnp.int32, (kc, 1), 0)

    def score_body(j, carry):
        c0 = pl.multiple_of(j * kc, kc)
        kx = auxk_ref[0, pl.ds(c0, kc), :][:, :IDX_DIM].astype(BF16)
        lg = _dot_nt(kx, qstack)
        sc = jnp.zeros((kc, tq), F32)
        for h in range(IDX_HEADS):
            sc = sc + wt[h:h + 1, :] * jnp.maximum(lg[:, h * tq:(h + 1) * tq], 0.0)
        bits = lax.bitcast_convert_type(sc, jnp.int32)
        key = bits ^ (lax.shift_right_arithmetic(bits, 31) & 0x7FFFFFFF)
        key = jnp.where(sc == 0.0, 0, key)
        key = jnp.where(c0 + sub_pos <= t, key, INT_MIN)
        key_scr[pl.ds(c0, kc), :] = key
        return carry

    lax.fori_loop(0, nkc, score_body, 0)

    def count(hits):
        def body(j, acc):
            c0 = pl.multiple_of(j * kc, kc)
            hit = hits(key_scr[pl.ds(c0, kc), :], c0 + sub_pos)
            return acc + jnp.sum(hit.reshape(kc // COUNT_ROWS, COUNT_ROWS, tq), axis=0)
        acc = lax.fori_loop(0, nkc, body, jnp.zeros((COUNT_ROWS, tq), F32))
        return jnp.sum(acc, axis=0, keepdims=True)

    kf = float(topk)

    tau = jnp.where(count(lambda key, pos: jnp.where(key >= 0, 1.0, 0.0)) >= kf, 0, INT_MIN).astype(jnp.int32)

    def bit_body(b, tau):
        cand = tau + lax.shift_left(jnp.int32(1), 30 - b)
        return jnp.where(count(lambda key, pos: jnp.where(key >= cand, 1.0, 0.0)) >= kf, cand, tau)

    tau = lax.fori_loop(0, 31, bit_body, tau)

    need = kf - count(lambda key, pos: jnp.where(key > tau, 1.0, 0.0))
    n_ge = count(lambda key, pos: jnp.where(key >= tau, 1.0, 0.0))
    jcut_scr[...] = jnp.full(jcut_scr.shape, s, jnp.int32)

    @pl.when(jnp.max(n_ge) > kf)
    def _():
        nbits = max(1, (s - 1).bit_length())

        def pos_body(b, lo):
            cand = lo + lax.shift_left(jnp.int32(1), nbits - 1 - b)
            below = count(lambda key, pos: jnp.where(key == tau, jnp.where(pos < cand, 1.0, 0.0), 0.0))
            return jnp.where(below < need, cand, lo)

        jcut_scr[...] = lax.fori_loop(0, nbits, pos_body, jnp.zeros((1, tq), jnp.int32))

    jcut = jcut_scr[...]

    def mask_body(j, carry):
        c0 = pl.multiple_of(j * kc, kc)
        key = key_scr[pl.ds(c0, kc), :]
        pos = c0 + sub_pos
        sel = jnp.where(key > tau, 1.0, jnp.where(key == tau, jnp.where(pos <= jcut, 1.0, 0.0), 0.0))
        sel = jnp.where(pos <= t, sel, 0.0)
        for g in range(kc // tq):
            tile = sel[g * tq:(g + 1) * tq, :].T
            mask_ref[0, :, pl.ds(pl.multiple_of(c0 + g * tq, tq), tq)] = tile.astype(jnp.int8)
        return carry

    lax.fori_loop(0, nkc, mask_body, 0)

    def zero_body(j, carry):
        c0 = pl.multiple_of(j * kc, kc)
        mask_ref[0, :, pl.ds(c0, kc)] = jnp.zeros((tq, kc), jnp.int8)
        return carry

    lax.fori_loop(nkc, s // kc, zero_body, 0)


def _dsa_select(qi, aux, topk):
    b, s, _ = qi.shape
    tq = Q_BLOCK
    return pl.pallas_call(
        functools.partial(_select_kernel, topk=topk),
        grid=(b, s // tq),
        in_specs=[
            pl.BlockSpec((1, tq, qi.shape[-1]), lambda i, j: (i, j, 0)),
            pl.BlockSpec((1, tq, LANES), lambda i, j: (i, j, 0)),
            pl.BlockSpec((1, s, LANES), lambda i, j: (i, 0, 0)),
        ],
        out_specs=pl.BlockSpec((1, tq, s), lambda i, j: (i, j, 0)),
        out_shape=jax.ShapeDtypeStruct((b, s, s), jnp.int8),
        scratch_shapes=[pltpu.VMEM((s, tq), jnp.int32), pltpu.VMEM((1, tq), jnp.int32)],
        compiler_params=_cparams("parallel", "parallel"),
        name="dsa_select",
    )(qi, aux, aux)


ATT_KB = 512
ATT_ROWS = 256
LOG2E = 1.4426950408889634


def _attn_kernel(q_ref, ckv_ref, mask_ref, wuk_ref, wuv_ref, o_ref,
                 qlat_scr, m_scr, l_scr, acc_scr):
    i = pl.program_id(1)
    tq = q_ref.shape[1]
    s = ckv_ref.shape[1]
    nh, dqk, dl = wuk_ref.shape
    dv = wuv_ref.shape[2]
    kb = min(ATT_KB, s)
    nkb = (i * tq + tq + kb - 1) // kb
    rg = ATT_ROWS
    hg = rg // tq

    q = q_ref[0]
    scale = (dqk ** -0.5) * LOG2E
    for h in range(nh):
        ql = _dot(q[:, h * dqk:(h + 1) * dqk], wuk_ref[h]) * scale
        qlat_scr[h * tq:(h + 1) * tq, :] = ql.astype(BF16)
    m_scr[...] = jnp.full(m_scr.shape, NEG_BIG, F32)
    l_scr[...] = jnp.zeros_like(l_scr)
    acc_scr[...] = jnp.zeros_like(acc_scr)
    n_groups = nh * tq // rg
    lane_tiles = kb // LANES

    def body(j, carry):
        c0 = pl.multiple_of(j * kb, kb)
        kv = ckv_ref[0, pl.ds(c0, kb), :]
        bias = ((mask_ref[0, :, pl.ds(c0, kb)].astype(F32) - 1.0) * (-NEG_BIG)).astype(BF16)

        def scores(g):
            return _dot_nt(qlat_scr[g * rg:(g + 1) * rg, :], kv).astype(BF16)

        sc_next = scores(0)
        for g in range(n_groups):
            rows = slice(g * rg, (g + 1) * rg)
            sc = sc_next
            if g + 1 < n_groups:
                sc_next = scores(g + 1)
            sc = (sc.reshape(hg, tq, kb) + bias[None]).reshape(rg, kb)
            part = sc[:, :LANES]
            for t in range(1, lane_tiles):
                part = jnp.maximum(part, sc[:, t * LANES:(t + 1) * LANES])
            m_old = m_scr[rows, :]
            m_new = jnp.maximum(m_old, jnp.max(part.astype(F32), axis=-1, keepdims=True))
            alpha = jnp.exp2(m_old - m_new)
            p = jnp.exp2(sc - jnp.tile(m_new.astype(BF16), (1, lane_tiles)))
            tiles = [p[:, t * LANES:(t + 1) * LANES] for t in range(lane_tiles)]
            while len(tiles) > 1:
                tiles = [a + b for a, b in zip(tiles[::2], tiles[1::2])]
            l_scr[rows, :] = alpha * l_scr[rows, :] + tiles[0].astype(F32)
            acc_scr[rows, :] = (jnp.tile(alpha, (1, dl // LANES)) * acc_scr[rows, :] + _dot(p, kv))
            m_scr[rows, :] = m_new
        return carry

    def pair_body(jj, carry):
        return body(2 * jj + 1, body(2 * jj, carry))

    lax.fori_loop(0, nkb // 2, pair_body, 0)

    @pl.when(nkb % 2 == 1)
    def _():
        body(nkb - 1, 0)

    l_tot = jnp.sum(l_scr[...], axis=-1, keepdims=True)
    o_lat = (acc_scr[...] / l_tot).astype(BF16)
    for h in range(nh):
        o_ref[0, :, h * dv:(h + 1) * dv] = _dot(o_lat[h * tq:(h + 1) * tq], wuv_ref[h]).astype(BF16)


def _dsa_attn(q, ckv, mask, wuk, wuv):
    b, s, _ = q.shape
    tq = Q_BLOCK
    nh, dqk, dl = wuk.shape
    dv = wuv.shape[2]
    return pl.pallas_call(
        _attn_kernel,
        grid=(b, s // tq),
        in_specs=[
            pl.BlockSpec((1, tq, q.shape[-1]), lambda i, j: (i, j, 0)),
            pl.BlockSpec((1, s, dl), lambda i, j: (i, 0, 0)),
            pl.BlockSpec((1, tq, s), lambda i, j: (i, j, 0)),
            _const_spec(wuk.shape),
            _const_spec(wuv.shape),
        ],
        out_specs=pl.BlockSpec((1, tq, nh * dv), lambda i, j: (i, j, 0)),
        out_shape=jax.ShapeDtypeStruct((b, s, nh * dv), BF16),
        scratch_shapes=[
            pltpu.VMEM((nh * tq, dl), BF16),
            pltpu.VMEM((nh * tq, LANES), F32),
            pltpu.VMEM((nh * tq, LANES), F32),
            pltpu.VMEM((nh * tq, dl), F32),
        ],
        compiler_params=_cparams("parallel", "parallel"),
        name="dsa_attn",
    )(q, ckv, mask, wuk, wuv)


def _gdn_layer(x, mod, ng, w_in, conv_w, a_log, dt_bias, norm_g, w_out, tm):
    b, s, _ = x.shape
    nmain = 2 * GDN_HEADS * GDN_DK + 2 * GDN_HEADS * GDN_DV
    proj, gates = _gdn_in(
        x, mod, ng, w_in[:, :nmain].astype(BF16), w_in[:, nmain:].astype(BF16),
        a_log.reshape(1, -1), dt_bias.reshape(1, -1), tm)
    grow = jnp.swapaxes(gates.reshape(b, s // GDN_CHUNK, GDN_CHUNK, -1), 2, 3)
    o = _gdn_core(proj, gates, grow, conv_w, norm_g.reshape(1, -1), min(tm, 256))
    return _outproj(o, x, mod, ng, w_out.astype(BF16), tm)


def _dsa_layer(x, mod, ng, w_in, qg, kvg, wuq, wuk, wuv, wqi, lng, lnb, w_out, tm):
    b, s, d = x.shape
    topk = min(TOPK_MAX, s // 4)
    n_in = w_in.shape[1]
    n_pad = DSA_Q_LORA + DSA_KV_LORA + LANES
    w_in_p = jnp.pad(w_in, ((0, 0), (0, n_pad - n_in))).astype(BF16)
    pad = LANES - IDX_DIM
    q, qi, ckv, aux = _dsa_in(
        x, mod, ng, w_in_p, qg.reshape(1, -1), kvg.reshape(1, -1),
        jnp.pad(lng, (0, pad)).reshape(1, -1), jnp.pad(lnb, (0, pad)).reshape(1, -1),
        wuq.astype(BF16), wqi.astype(BF16), tm)
    mask = _dsa_select(qi, aux, topk)
    o = _dsa_attn(q, ckv, mask, wuk.astype(BF16), wuv.astype(BF16))
    return _outproj(o, x, mod, ng, w_out.astype(BF16), tm)


def kernel(x, c, ada_w, ada_b, norm_g, ffn_w_in, ffn_w_out, gdn_w_in, gdn_conv_w, gdn_a_log,
           gdn_dt_bias, gdn_norm_g, gdn_w_out, dsa_w_in, dsa_q_norm_g, dsa_kv_norm_g, dsa_w_uq,
           dsa_w_uk, dsa_w_uv, dsa_w_qidx, dsa_kidx_ln_g, dsa_kidx_ln_b, dsa_w_out):
    b, s, d = x.shape
    depth = ada_w.shape[0]
    tm = min(512, s)
    fh = ffn_w_out.shape[1]
    fc = 256
    mods = _adaln_mod(c, ada_w, ada_b).reshape(depth, b, 6, d)
    for layer in range(depth):
        mod = mods[layer]
        ng = norm_g[layer]
        j = layer // 2
        if layer % 2 == 0:
            x = _gdn_layer(x, mod, ng, gdn_w_in[j], gdn_conv_w[j], gdn_a_log[j], gdn_dt_bias[j],
                           gdn_norm_g[j], gdn_w_out[j], tm)
        else:
            x = _dsa_layer(x, mod, ng, dsa_w_in[j], dsa_q_norm_g[j], dsa_kv_norm_g[j],
                           dsa_w_uq[j], dsa_w_uk[j], dsa_w_uv[j], dsa_w_qidx[j],
                           dsa_kidx_ln_g[j], dsa_kidx_ln_b[j], dsa_w_out[j], tm)
        w_in = ffn_w_in[layer].astype(BF16)
        wg = jnp.swapaxes(w_in[:, :fh].reshape(d, fh // fc, fc), 0, 1)
        wu = jnp.swapaxes(w_in[:, fh:].reshape(d, fh // fc, fc), 0, 1)
        wo = ffn_w_out[layer].astype(BF16).reshape(fh // fc, fc, d)
        x = _ffn(x, mod, ng, wg, wu, wo, tm)
    return x
```

```python
import functools

import jax
import jax.numpy as jnp
from jax import lax
from jax.experimental import pallas as pl
from jax.experimental.pallas import tpu as pltpu

NORM_EPS = 1e-6
L2_EPS = 1e-6

GDN_HEADS = 8
GDN_DK = 128
GDN_DV = 128
GDN_CONV = 4
GDN_CHUNK = 64

DSA_HEADS = 16
DSA_QK_DIM = 64
DSA_V_DIM = 64
DSA_Q_LORA = 256
DSA_KV_LORA = 256
IDX_HEADS = 8
IDX_DIM = 64
TOPK_MAX = 256
Q_BLOCK = 128

LANES = 128
VMEM_LIMIT = 56 * 1024 * 1024
NEG_BIG = -(2.0 ** 100)
INT_MIN = -(2 ** 31)

F32 = jnp.float32
BF16 = jnp.bfloat16
HIGHEST = lax.Precision.HIGHEST


def _cparams(*sem):
    return pltpu.CompilerParams(dimension_semantics=sem, vmem_limit_bytes=VMEM_LIMIT)


def _dot(a, b, precision=None):
    return jnp.dot(a, b, preferred_element_type=F32, precision=precision)


def _dot_nt(a, b, precision=None):
    return lax.dot_general(a, b, (((1,), (1,)), ((), ())),
                           preferred_element_type=F32, precision=precision)


def _dot_tn(a, b, precision=None):
    return lax.dot_general(a, b, (((0,), (0,)), ((), ())),
                           preferred_element_type=F32, precision=precision)


def _bdot(a, b):
    return _dot(a.astype(BF16), b.astype(BF16))


def _silu(x):
    return x * jax.nn.sigmoid(x)


def _rms(x, g):
    return x * lax.rsqrt(jnp.mean(x * x, axis=-1, keepdims=True) + NORM_EPS) * g


def _const_spec(shape, single_buffer=False):
    zeros = (0,) * len(shape)
    mode = pl.Buffered(1) if single_buffer else None
    return pl.BlockSpec(shape, lambda *_: zeros, pipeline_mode=mode)


def _mod_kernel(c_ref, w_ref, b_ref, o_ref):
    c = c_ref[...]
    o_ref[0] = _dot(_silu(c), w_ref[0], precision=HIGHEST) + b_ref[0]


def _adaln_mod(c, ada_w, ada_b):
    depth, d, n = ada_w.shape
    b = c.shape[0]
    tn = n // 4
    return pl.pallas_call(
        _mod_kernel,
        grid=(depth, n // tn),
        in_specs=[
            pl.BlockSpec((b, d), lambda l, j: (0, 0)),
            pl.BlockSpec((1, d, tn), lambda l, j: (l, 0, j)),
            pl.BlockSpec((1, 1, tn), lambda l, j: (l, 0, j)),
        ],
        out_specs=pl.BlockSpec((1, b, tn), lambda l, j: (l, 0, j)),
        out_shape=jax.ShapeDtypeStruct((depth, b, n), F32),
        compiler_params=_cparams("parallel", "parallel"),
        name="adaln_mod",
    )(c, ada_w, ada_b.reshape(depth, 1, n))


def _norm_mod(x, g, scale, shift):
    return _rms(x, g) * (1.0 + scale) + shift


FFN_COLS = 256


def _mix_ffn_kernel(o_ref, x_ref, mod_ref, ng_ref, wmix_ref, win_ref, wout_ref, out_ref,
                    h_scr, xmid_scr, acc_scr):
    m = mod_ref[0]
    ng = ng_ref[...]
    fh = wout_ref.shape[0]
    y = _dot(o_ref[0], wmix_ref[...])
    x = x_ref[0] + (1.0 + m[2:3]) * _rms(y, ng[1:2])
    h_scr[...] = _norm_mod(x, ng[2:3], m[4:5], m[3:4]).astype(BF16)
    xmid_scr[...] = x
    acc_scr[...] = jnp.zeros_like(acc_scr)
    for j in range(fh // FFN_COLS):
        cols = slice(j * FFN_COLS, (j + 1) * FFN_COLS)
        h = h_scr[...]
        gate = _dot(h, win_ref[:, cols])
        up = _dot(h, win_ref[:, fh + j * FFN_COLS:fh + (j + 1) * FFN_COLS])
        acc_scr[...] += _dot((_silu(gate) * up).astype(BF16), wout_ref[cols, :])
    out_ref[0] = xmid_scr[...] + (1.0 + m[5:6]) * _rms(acc_scr[...], ng[3:4])


def _mix_ffn(o, x, mod, ng, w_mix, w_in, w_out, tm):
    b, s, d = x.shape
    return pl.pallas_call(
        _mix_ffn_kernel,
        grid=(b, s // tm),
        in_specs=[
            pl.BlockSpec((1, tm, o.shape[-1]), lambda i, j: (i, j, 0)),
            pl.BlockSpec((1, tm, d), lambda i, j: (i, j, 0)),
            pl.BlockSpec((1, 6, d), lambda i, j: (i, 0, 0)),
            _const_spec(ng.shape),
            _const_spec(w_mix.shape, single_buffer=True),
            _const_spec(w_in.shape, single_buffer=True),
            _const_spec(w_out.shape, single_buffer=True),
        ],
        out_specs=pl.BlockSpec((1, tm, d), lambda i, j: (i, j, 0)),
        out_shape=jax.ShapeDtypeStruct(x.shape, F32),
        scratch_shapes=[pltpu.VMEM((tm, d), BF16), pltpu.VMEM((tm, d), F32),
                        pltpu.VMEM((tm, d), F32)],
        compiler_params=_cparams("parallel", "parallel"),
        name="mix_ffn",
    )(o, x, mod, ng, w_mix, w_in, w_out)


def _gdn_in_kernel(x_ref, mod_ref, ng_ref, w_ref, wgate_ref, alog_ref, dtb_ref,
                   proj_ref, gates_ref, h_scr):
    x = x_ref[0]
    m = mod_ref[0]
    tm = x.shape[0]
    h_scr[...] = _norm_mod(x, ng_ref[0:1], m[1:2], m[0:1]).astype(BF16)
    h = h_scr[...]
    n = w_ref.shape[1]
    tn = 512
    for j in range(n // tn):
        proj_ref[0, :, j * tn:(j + 1) * tn] = _dot(h, w_ref[:, j * tn:(j + 1) * tn]).astype(BF16)
    raw = _dot(h, wgate_ref[...])
    nh = raw.shape[1] // 2
    beta = jax.nn.sigmoid(raw[:, :nh])
    a = raw[:, nh:] + dtb_ref[...]
    softplus = jnp.maximum(a, 0.0) + jnp.log1p(jnp.exp(-jnp.abs(a)))
    g = -jnp.exp(alog_ref[...]) * softplus
    r = lax.broadcasted_iota(jnp.int32, (tm, tm), 0)
    c = lax.broadcasted_iota(jnp.int32, (tm, tm), 1)
    shift = GDN_CHUNK.bit_length() - 1
    same_chunk = jnp.right_shift(r, shift) == jnp.right_shift(c, shift)
    mcum = jnp.where(same_chunk, jnp.where(r >= c, 1.0, 0.0), 0.0).astype(BF16)
    g_hi = g.astype(BF16)
    g_mid = (g - g_hi.astype(F32)).astype(BF16)
    g_lo = (g - g_hi.astype(F32) - g_mid.astype(F32)).astype(BF16)
    parts = _dot(mcum, jnp.concatenate([g_hi, g_mid, g_lo], axis=1))
    gcum = parts[:, :nh] + (parts[:, nh:2 * nh] + parts[:, 2 * nh:])
    gates_ref[0, :, 0:nh] = beta
    gates_ref[0, :, nh:2 * nh] = gcum


def _gdn_in(x, mod, ng, w, wgate, a_log, dt_bias, tm):
    b, s, d = x.shape
    n = w.shape[1]
    nh2 = wgate.shape[1]
    return pl.pallas_call(
        _gdn_in_kernel,
        grid=(b, s // tm),
        in_specs=[
            pl.BlockSpec((1, tm, d), lambda i, j: (i, j, 0)),
            pl.BlockSpec((1, 6, d), lambda i, j: (i, 0, 0)),
            _const_spec(ng.shape),
            _const_spec(w.shape),
            _const_spec(wgate.shape),
            _const_spec(a_log.shape),
            _const_spec(dt_bias.shape),
        ],
        out_specs=[
            pl.BlockSpec((1, tm, n), lambda i, j: (i, j, 0)),
            pl.BlockSpec((1, tm, nh2), lambda i, j: (i, j, 0)),
        ],
        out_shape=[
            jax.ShapeDtypeStruct((b, s, n), BF16),
            jax.ShapeDtypeStruct((b, s, nh2), F32),
        ],
        scratch_shapes=[pltpu.VMEM((tm, d), BF16)],
        compiler_params=_cparams("parallel", "parallel"),
        name="gdn_in",
    )(x, mod, ng, w, wgate, a_log, dt_bias)


SOLVE_BASE = 8
GDN_CHUNKS_PER_STEP = 4


def _gdn_kernel(proj_ref, gcol_ref, grow_ref, convw_ref, ng_ref, shift_ref, out_ref,
                conv_scr, carry_scr, state_scr):
    tb = proj_ref.shape[1]
    nh = GDN_HEADS
    dk = GDN_DK
    cs = GDN_CHUNK
    hk = nh * dk
    nqkv = conv_scr.shape[1]

    @pl.when(pl.program_id(1) == 0)
    def _():
        carry_scr[...] = jnp.zeros_like(carry_scr)
        state_scr[...] = jnp.zeros_like(state_scr)

    cw = 512
    row8 = lax.broadcasted_iota(jnp.int32, (8, cw), 0)
    for j in range(nqkv // cw):
        cols = slice(j * cw, (j + 1) * cw)
        u = proj_ref[0, :, cols]
        shifted = _dot(shift_ref[...], u)
        prev = carry_scr[:, cols]
        w = convw_ref[:, cols]
        raw = u.astype(F32)
        y = w[GDN_CONV - 1:GDN_CONV] * raw
        head = y[0:8]
        for d in range(1, GDN_CONV):
            wd = w[GDN_CONV - 1 - d:GDN_CONV - d]
            y = y + wd * shifted[(d - 1) * tb:d * tb]
            head = head + wd * (shifted[(d - 1) * tb:(d - 1) * tb + 8]
                                + jnp.where(row8 < d, pltpu.roll(prev, d, axis=0), 0.0))
        conv_scr[:, cols] = _silu(y)
        conv_scr[0:8, cols] = _silu(head)
        carry_scr[:, cols] = raw[tb - 8:tb]

    ri = lax.broadcasted_iota(jnp.int32, (cs, cs), 0)
    ci = lax.broadcasted_iota(jnp.int32, (cs, cs), 1)
    lower = ri >= ci
    strict = ri > ci
    ng = ng_ref[...]

    heads = range(nh)
    eye = jnp.where(ri == ci, 1.0, 0.0)

    def same_block(width):
        shift = width.bit_length() - 1
        return jnp.right_shift(ri, shift) == jnp.right_shift(ci, shift)

    base_blocks = same_block(SOLVE_BASE)
    merge_masks = []
    width = SOLVE_BASE
    while width < cs:
        merge_masks.append((same_block(width), same_block(2 * width)))
        width *= 2

    nc = GDN_CHUNKS_PER_STEP
    units = [(cc, h) for cc in range(nc) for h in heads]
    every = range(len(units))

    def chunk_body(c, carry):
        qs, ks, rs, decays, qgs, kgs, glasts, rowss = [], [], [], [], [], [], [], []
        for cc, h in units:
            r0 = pl.multiple_of((c * nc + cc) * cs, cs)
            rows = pl.ds(r0, cs)
            gc = gcol_ref[0, rows, :]
            gr = grow_ref[0, c * nc + cc]
            rowss.append(rows)
            q = conv_scr[rows, h * dk:(h + 1) * dk]
            k = conv_scr[rows, hk + h * dk:hk + (h + 1) * dk]
            v = conv_scr[rows, 2 * hk + h * dk:2 * hk + (h + 1) * dk]
            q = q * (lax.rsqrt(jnp.sum(q * q, axis=-1, keepdims=True) + L2_EPS) * (dk ** -0.5))
            k = k * lax.rsqrt(jnp.sum(k * k, axis=-1, keepdims=True) + L2_EPS)
            beta = gc[:, h:h + 1]
            gcum = gc[:, nh + h:nh + h + 1]
            grow = gr[nh + h:nh + h + 1, :]
            glast = gcum[cs - 1:cs, :]
            eg = jnp.exp(gcum)
            kb = k * beta
            qs.append(jnp.concatenate([kb, q], axis=0).astype(BF16))
            ks.append(k.astype(BF16))
            rs.append(jnp.concatenate([v * beta, kb * eg], axis=1))
            decays.append(jnp.where(lower, jnp.exp(jnp.minimum(gcum - grow, 0.0)), 0.0))
            qgs.append((q * eg).astype(BF16))
            kgs.append((k * jnp.exp(glast - gcum)).astype(BF16))
            glasts.append(glast)
        lows, attns = [], []
        for u in every:
            x = _dot_nt(qs[u], ks[u])
            lows.append(jnp.where(strict, x[:cs] * decays[u], 0.0))
            attns.append((x[cs:] * decays[u]).astype(BF16))
        ts, pws = [], []
        for u in every:
            m = -jnp.where(base_blocks, lows[u], 0.0)
            ts.append(eye + m)
            pws.append(m)
        for j in range(1, (SOLVE_BASE - 1).bit_length()):
            for u in every:
                pws[u] = _bdot(pws[u], pws[u])
                ts[u] = ts[u] + _bdot(ts[u], pws[u])
        for inner, outer in merge_masks:
            for u in every:
                tb16 = ts[u].astype(BF16)
                coupling = jnp.where(outer, jnp.where(inner, 0.0, lows[u]), 0.0)
                ts[u] = ts[u] - _bdot(tb16, _dot(coupling.astype(BF16), tb16))
        for u in every:
            rs[u] = _bdot(ts[u], rs[u])
        for cc in range(nc):
            v_news, sts = [], []
            for h in heads:
                u = cc * nh + h
                st = state_scr[h]
                x = _dot(jnp.concatenate([rs[u][:, GDN_DV:].astype(BF16), qgs[u]], axis=0),
                         st.astype(BF16))
                v_news.append((rs[u][:, :GDN_DV] - x[:cs]).astype(BF16))
                qgs[u] = x[cs:]
                sts.append(st)
            for h in heads:
                u = cc * nh + h
                o = qgs[u] + _dot(attns[u], v_news[h])
                state_scr[h] = sts[h] * jnp.exp(glasts[u]) + _dot_tn(kgs[u], v_news[h])
                z = proj_ref[0, rowss[u], 3 * hk + h * dk:3 * hk + (h + 1) * dk].astype(F32)
                out_ref[0, rowss[u], h * dk:(h + 1) * dk] = (_rms(o, ng) * _silu(z)).astype(BF16)
        return carry

    lax.fori_loop(0, tb // (cs * nc), chunk_body, 0)


def _gdn_core(proj, gates, grow, conv_w, norm_g, tb):
    b, s, n = proj.shape
    hv = GDN_HEADS * GDN_DV
    nqkv = conv_w.shape[1]
    t_out = jnp.arange(tb)[:, None]
    t_in = jnp.arange(tb)[None, :]
    shift = jnp.concatenate([(t_out - d == t_in) for d in range(1, GDN_CONV)], axis=0).astype(BF16)
    return pl.pallas_call(
        _gdn_kernel,
        grid=(b, s // tb),
        in_specs=[
            pl.BlockSpec((1, tb, n), lambda i, j: (i, j, 0)),
            pl.BlockSpec((1, tb, gates.shape[-1]), lambda i, j: (i, j, 0)),
            pl.BlockSpec((1, tb // GDN_CHUNK) + grow.shape[2:], lambda i, j: (i, j, 0, 0)),
            _const_spec(conv_w.shape),
            _const_spec(norm_g.shape),
            _const_spec(shift.shape),
        ],
        out_specs=pl.BlockSpec((1, tb, hv), lambda i, j: (i, j, 0)),
        out_shape=jax.ShapeDtypeStruct((b, s, hv), BF16),
        scratch_shapes=[
            pltpu.VMEM((tb, nqkv), F32),
            pltpu.VMEM((8, nqkv), F32),
            pltpu.VMEM((GDN_HEADS, GDN_DK, GDN_DV), F32),
        ],
        compiler_params=_cparams("parallel", "arbitrary"),
        name="gdn_core",
    )(proj, gates, grow, conv_w, norm_g, shift)


def _dsa_in_kernel(x_ref, mod_ref, ng_ref, w_ref, qg_ref, kvg_ref, lng_ref, lnb_ref,
                   wuq_ref, wqi_ref, q_ref, qi_ref, ckv_ref, aux_ref):
    x = x_ref[0]
    m = mod_ref[0]
    h = _norm_mod(x, ng_ref[0:1], m[1:2], m[0:1]).astype(BF16)
    proj = _dot(h, w_ref[...])
    ql, kvl = DSA_Q_LORA, DSA_KV_LORA
    c_q = _rms(proj[:, :ql], qg_ref[...]).astype(BF16)
    ckv_ref[0] = _rms(proj[:, ql:ql + kvl], kvg_ref[...]).astype(BF16)
    q_ref[0] = _dot(c_q, wuq_ref[...]).astype(BF16)
    qi_ref[0] = _dot(c_q, wqi_ref[...]).astype(BF16)
    tail = proj[:, ql + kvl:ql + kvl + LANES]
    lane = lax.broadcasted_iota(jnp.int32, tail.shape, 1)
    is_key = lane < IDX_DIM
    mu = jnp.sum(jnp.where(is_key, tail, 0.0), axis=-1, keepdims=True) * (1.0 / IDX_DIM)
    dev = jnp.where(is_key, tail - mu, 0.0)
    var = jnp.sum(dev * dev, axis=-1, keepdims=True) * (1.0 / IDX_DIM)
    kn = dev * lax.rsqrt(var + NORM_EPS) * lng_ref[...] + lnb_ref[...]
    wscale = (IDX_HEADS ** -0.5) * (IDX_DIM ** -0.5)
    aux_ref[0] = jnp.where(is_key, kn, tail * wscale)


def _dsa_in(x, mod, ng, w, qg, kvg, lng, lnb, wuq, wqi, tm):
    b, s, d = x.shape
    outs = [
        (wuq.shape[1], BF16),
        (wqi.shape[1], BF16),
        (DSA_KV_LORA, BF16),
        (LANES, F32),
    ]
    return pl.pallas_call(
        _dsa_in_kernel,
        grid=(b, s // tm),
        in_specs=[
            pl.BlockSpec((1, tm, d), lambda i, j: (i, j, 0)),
            pl.BlockSpec((1, 6, d), lambda i, j: (i, 0, 0)),
            _const_spec(ng.shape),
            _const_spec(w.shape),
            _const_spec(qg.shape),
            _const_spec(kvg.shape),
            _const_spec(lng.shape),
            _const_spec(lnb.shape),
            _const_spec(wuq.shape),
            _const_spec(wqi.shape),
        ],
        out_specs=[pl.BlockSpec((1, tm, n), lambda i, j: (i, j, 0)) for n, _ in outs],
        out_shape=[jax.ShapeDtypeStruct((b, s, n), dt) for n, dt in outs],
        compiler_params=_cparams("parallel", "parallel"),
        name="dsa_in",
    )(x, mod, ng, w, qg, kvg, lng, lnb, wuq, wqi)


SEL_KC = 512
COUNT_ROWS = 32


def _select_kernel(qi_ref, auxq_ref, auxk_ref, mask_ref, key_scr, jcut_scr, *, topk):
    i = pl.program_id(1)
    tq = qi_ref.shape[1]
    s = auxk_ref.shape[1]
    kc = min(SEL_KC, s)
    nkc = (i * tq + tq + kc - 1) // kc
    q = qi_ref[0]
    qstack = jnp.concatenate([q[:, h * IDX_DIM:(h + 1) * IDX_DIM] for h in range(IDX_HEADS)], axis=0)
    wt = auxq_ref[0].T[IDX_DIM:IDX_DIM + IDX_HEADS, :]
    t = i * tq + lax.broadcasted_iota(jnp.int32, (1, tq), 1)
    sub_pos = lax.broadcasted_iota(jnp.int32, (kc, 1), 0)

    def score_body(j, carry):
        c0 = pl.multiple_of(j * kc, kc)
        kx = auxk_ref[0, pl.ds(c0, kc), :][:, :IDX_DIM].astype(BF16)
        lg = _dot_nt(kx, qstack)
        sc = jnp.zeros((kc, tq), F32)
        for h in range(IDX_HEADS):
            sc = sc + wt[h:h + 1, :] * jnp.maximum(lg[:, h * tq:(h + 1) * tq], 0.0)
        bits = lax.bitcast_convert_type(sc, jnp.int32)
        key = bits ^ (lax.shift_right_arithmetic(bits, 31) & 0x7FFFFFFF)
        key = jnp.where(sc == 0.0, 0, key)
        key = jnp.where(c0 + sub_pos <= t, key, INT_MIN)
        key_scr[pl.ds(c0, kc), :] = key
        return carry

    lax.fori_loop(0, nkc, score_body, 0)

    def count(hits):
        def body(j, acc):
            c0 = pl.multiple_of(j * kc, kc)
            hit = hits(key_scr[pl.ds(c0, kc), :], c0 + sub_pos)
            return acc + jnp.sum(hit.reshape(kc // COUNT_ROWS, COUNT_ROWS, tq), axis=0)
        acc = lax.fori_loop(0, nkc, body, jnp.zeros((COUNT_ROWS, tq), F32))
        return jnp.sum(acc, axis=0, keepdims=True)

    kf = float(topk)

    tau = jnp.where(count(lambda key, pos: jnp.where(key >= 0, 1.0, 0.0)) >= kf, 0, INT_MIN).astype(jnp.int32)

    def bit_body(b, tau):
        cand = tau + lax.shift_left(jnp.int32(1), 30 - b)
        return jnp.where(count(lambda key, pos: jnp.where(key >= cand, 1.0, 0.0)) >= kf, cand, tau)

    tau = lax.fori_loop(0, 31, bit_body, tau)

    need = kf - count(lambda key, pos: jnp.where(key > tau, 1.0, 0.0))
    n_ge = count(lambda key, pos: jnp.where(key >= tau, 1.0, 0.0))
    jcut_scr[...] = jnp.full(jcut_scr.shape, s, jnp.int32)

    @pl.when(jnp.max(n_ge) > kf)
    def _():
        nbits = max(1, (s - 1).bit_length())

        def pos_body(b, lo):
            cand = lo + lax.shift_left(jnp.int32(1), nbits - 1 - b)
            below = count(lambda key, pos: jnp.where(key == tau, jnp.where(pos < cand, 1.0, 0.0), 0.0))
            return jnp.where(below < need, cand, lo)

        jcut_scr[...] = lax.fori_loop(0, nbits, pos_body, jnp.zeros((1, tq), jnp.int32))

    jcut = jcut_scr[...]

    def mask_body(j, carry):
        c0 = pl.multiple_of(j * kc, kc)
        key = key_scr[pl.ds(c0, kc), :]
        pos = c0 + sub_pos
        sel = jnp.where(key > tau, 1.0, jnp.where(key == tau, jnp.where(pos <= jcut, 1.0, 0.0), 0.0))
        sel = jnp.where(pos <= t, sel, 0.0)
        for g in range(kc // tq):
            tile = sel[g * tq:(g + 1) * tq, :].T
            mask_ref[0, :, pl.ds(pl.multiple_of(c0 + g * tq, tq), tq)] = tile.astype(jnp.int8)
        return carry

    lax.fori_loop(0, nkc, mask_body, 0)

    def zero_body(j, carry):
        c0 = pl.multiple_of(j * kc, kc)
        mask_ref[0, :, pl.ds(c0, kc)] = jnp.zeros((tq, kc), jnp.int8)
        return carry

    lax.fori_loop(nkc, s // kc, zero_body, 0)


def _dsa_select(qi, aux, topk):
    b, s, _ = qi.shape
    tq = Q_BLOCK
    return pl.pallas_call(
        functools.partial(_select_kernel, topk=topk),
        grid=(b, s // tq),
        in_specs=[
            pl.BlockSpec((1, tq, qi.shape[-1]), lambda i, j: (i, j, 0)),
            pl.BlockSpec((1, tq, LANES), lambda i, j: (i, j, 0)),
            pl.BlockSpec((1, s, LANES), lambda i, j: (i, 0, 0)),
        ],
        out_specs=pl.BlockSpec((1, tq, s), lambda i, j: (i, j, 0)),
        out_shape=jax.ShapeDtypeStruct((b, s, s), jnp.int8),
        scratch_shapes=[pltpu.VMEM((s, tq), jnp.int32), pltpu.VMEM((1, tq), jnp.int32)],
        compiler_params=_cparams("parallel", "parallel"),
        name="dsa_select",
    )(qi, aux, aux)


ATT_KB = 512
ATT_ROWS = 256
LOG2E = 1.4426950408889634


def _attn_kernel(q_ref, ckv_ref, mask_ref, wuk_ref, wuv_ref, o_ref,
                 qlat_scr, m_scr, l_scr, acc_scr):
    i = pl.program_id(1)
    tq = q_ref.shape[1]
    s = ckv_ref.shape[1]
    nh, dqk, dl = wuk_ref.shape
    dv = wuv_ref.shape[2]
    kb = min(ATT_KB, s)
    nkb = (i * tq + tq + kb - 1) // kb
    rg = ATT_ROWS
    hg = rg // tq

    q = q_ref[0]
    scale = (dqk ** -0.5) * LOG2E
    for h in range(nh):
        ql = _dot(q[:, h * dqk:(h + 1) * dqk], wuk_ref[h]) * scale
        qlat_scr[h * tq:(h + 1) * tq, :] = ql.astype(BF16)
    m_scr[...] = jnp.full(m_scr.shape, NEG_BIG, F32)
    l_scr[...] = jnp.zeros_like(l_scr)
    acc_scr[...] = jnp.zeros_like(acc_scr)
    n_groups = nh * tq // rg
    lane_tiles = kb // LANES

    def body(j, carry):
        c0 = pl.multiple_of(j * kb, kb)
        kv = ckv_ref[0, pl.ds(c0, kb), :]
        bias = ((mask_ref[0, :, pl.ds(c0, kb)].astype(F32) - 1.0) * (-NEG_BIG)).astype(BF16)

        def scores(g):
            return _dot_nt(qlat_scr[g * rg:(g + 1) * rg, :], kv).astype(BF16)

        sc_next = scores(0)
        for g in range(n_groups):
            rows = slice(g * rg, (g + 1) * rg)
            sc = sc_next
            if g + 1 < n_groups:
                sc_next = scores(g + 1)
            sc = (sc.reshape(hg, tq, kb) + bias[None]).reshape(rg, kb)
            part = sc[:, :LANES]
            for t in range(1, lane_tiles):
                part = jnp.maximum(part, sc[:, t * LANES:(t + 1) * LANES])
            m_old = m_scr[rows, :]
            m_new = jnp.maximum(m_old, jnp.max(part.astype(F32), axis=-1, keepdims=True))
            alpha = jnp.exp2(m_old - m_new)
            p = jnp.exp2(sc - jnp.tile(m_new.astype(BF16), (1, lane_tiles)))
            tiles = [p[:, t * LANES:(t + 1) * LANES] for t in range(lane_tiles)]
            while len(tiles) > 1:
                tiles = [a + b for a, b in zip(tiles[::2], tiles[1::2])]
            l_scr[rows, :] = alpha * l_scr[rows, :] + tiles[0].astype(F32)
            acc_scr[rows, :] = (jnp.tile(alpha, (1, dl // LANES)) * acc_scr[rows, :] + _dot(p, kv))
            m_scr[rows, :] = m_new
        return carry

    def pair_body(jj, carry):
        return body(2 * jj + 1, body(2 * jj, carry))

    lax.fori_loop(0, nkb // 2, pair_body, 0)

    @pl.when(nkb % 2 == 1)
    def _():
        body(nkb - 1, 0)

    inv_l = 1.0 / jnp.sum(l_scr[...], axis=-1, keepdims=True)
    for h in range(nh):
        rows = slice(h * tq, (h + 1) * tq)
        o_h = _dot(acc_scr[rows, :].astype(BF16), wuv_ref[h]) * inv_l[rows]
        o_ref[0, :, h * dv:(h + 1) * dv] = o_h.astype(BF16)


def _dsa_attn(q, ckv, mask, wuk, wuv):
    b, s, _ = q.shape
    tq = Q_BLOCK
    nh, dqk, dl = wuk.shape
    dv = wuv.shape[2]
    return pl.pallas_call(
        _attn_kernel,
        grid=(b, s // tq),
        in_specs=[
            pl.BlockSpec((1, tq, q.shape[-1]), lambda i, j: (i, j, 0)),
            pl.BlockSpec((1, s, dl), lambda i, j: (i, 0, 0)),
            pl.BlockSpec((1, tq, s), lambda i, j: (i, j, 0)),
            _const_spec(wuk.shape),
            _const_spec(wuv.shape),
        ],
        out_specs=pl.BlockSpec((1, tq, nh * dv), lambda i, j: (i, j, 0)),
        out_shape=jax.ShapeDtypeStruct((b, s, nh * dv), BF16),
        scratch_shapes=[
            pltpu.VMEM((nh * tq, dl), BF16),
            pltpu.VMEM((nh * tq, LANES), F32),
            pltpu.VMEM((nh * tq, LANES), F32),
            pltpu.VMEM((nh * tq, dl), F32),
        ],
        compiler_params=_cparams("parallel", "parallel"),
        name="dsa_attn",
    )(q, ckv, mask, wuk, wuv)


def _gdn_mixer(x, mod, ng, w_in, conv_w, a_log, dt_bias, norm_g, tm):
    b, s, _ = x.shape
    nmain = 2 * GDN_HEADS * GDN_DK + 2 * GDN_HEADS * GDN_DV
    proj, gates = _gdn_in(
        x, mod, ng, w_in[:, :nmain].astype(BF16), w_in[:, nmain:].astype(BF16),
        a_log.reshape(1, -1), dt_bias.reshape(1, -1), tm)
    grow = jnp.swapaxes(gates.reshape(b, s // GDN_CHUNK, GDN_CHUNK, -1), 2, 3)
    return _gdn_core(proj, gates, grow, conv_w, norm_g.reshape(1, -1), min(tm, 256))


def _dsa_mixer(x, mod, ng, w_in, qg, kvg, wuq, wuk, wuv, wqi, lng, lnb, tm):
    b, s, d = x.shape
    topk = min(TOPK_MAX, s // 4)
    n_in = w_in.shape[1]
    n_pad = DSA_Q_LORA + DSA_KV_LORA + LANES
    w_in_p = jnp.pad(w_in, ((0, 0), (0, n_pad - n_in))).astype(BF16)
    pad = LANES - IDX_DIM
    q, qi, ckv, aux = _dsa_in(
        x, mod, ng, w_in_p, qg.reshape(1, -1), kvg.reshape(1, -1),
        jnp.pad(lng, (0, pad)).reshape(1, -1), jnp.pad(lnb, (0, pad)).reshape(1, -1),
        wuq.astype(BF16), wqi.astype(BF16), tm)
    mask = _dsa_select(qi, aux, topk)
    return _dsa_attn(q, ckv, mask, wuk.astype(BF16), wuv.astype(BF16))


def kernel(x, c, ada_w, ada_b, norm_g, ffn_w_in, ffn_w_out, gdn_w_in, gdn_conv_w, gdn_a_log,
           gdn_dt_bias, gdn_norm_g, gdn_w_out, dsa_w_in, dsa_q_norm_g, dsa_kv_norm_g, dsa_w_uq,
           dsa_w_uk, dsa_w_uv, dsa_w_qidx, dsa_kidx_ln_g, dsa_kidx_ln_b, dsa_w_out):
    b, s, d = x.shape
    depth = ada_w.shape[0]
    tm = min(512, s)
    mods = _adaln_mod(c, ada_w, ada_b).reshape(depth, b, 6, d)
    for layer in range(depth):
        mod = mods[layer]
        ng = norm_g[layer]
        j = layer // 2
        if layer % 2 == 0:
            o = _gdn_mixer(x, mod, ng, gdn_w_in[j], gdn_conv_w[j], gdn_a_log[j], gdn_dt_bias[j],
                           gdn_norm_g[j], tm)
            w_mix = gdn_w_out[j]
        else:
            o = _dsa_mixer(x, mod, ng, dsa_w_in[j], dsa_q_norm_g[j], dsa_kv_norm_g[j],
                           dsa_w_uq[j], dsa_w_uk[j], dsa_w_uv[j], dsa_w_qidx[j],
                           dsa_kidx_ln_g[j], dsa_kidx_ln_b[j], tm)
            w_mix = dsa_w_out[j]
        x = _mix_ffn(o, x, mod, ng, w_mix.astype(BF16), ffn_w_in[layer].astype(BF16),
                     ffn_w_out[layer].astype(BF16), tm)
    return x
```

```python
import functools

import jax
import jax.numpy as jnp
from jax import lax
from jax.experimental import pallas as pl
from jax.experimental.pallas import tpu as pltpu

NORM_EPS = 1e-6
L2_EPS = 1e-6

GDN_HEADS = 8
GDN_DK = 128
GDN_DV = 128
GDN_CONV = 4
GDN_CHUNK = 64

DSA_HEADS = 16
DSA_QK_DIM = 64
DSA_V_DIM = 64
DSA_Q_LORA = 256
DSA_KV_LORA = 256
IDX_HEADS = 8
IDX_DIM = 64
TOPK_MAX = 256
Q_BLOCK = 128

LANES = 128
VMEM_LIMIT = 56 * 1024 * 1024
NEG_BIG = -(2.0 ** 100)
INT_MIN = -(2 ** 31)

F32 = jnp.float32
BF16 = jnp.bfloat16
HIGHEST = lax.Precision.HIGHEST


def _cparams(*sem):
    return pltpu.CompilerParams(dimension_semantics=sem, vmem_limit_bytes=VMEM_LIMIT)


def _dot(a, b, precision=None):
    return jnp.dot(a, b, preferred_element_type=F32, precision=precision)


def _dot_nt(a, b, precision=None):
    return lax.dot_general(a, b, (((1,), (1,)), ((), ())),
                           preferred_element_type=F32, precision=precision)


def _dot_tn(a, b, precision=None):
    return lax.dot_general(a, b, (((0,), (0,)), ((), ())),
                           preferred_element_type=F32, precision=precision)


def _bdot(a, b):
    return _dot(a.astype(BF16), b.astype(BF16))


def _silu(x):
    return x * jax.nn.sigmoid(x)


def _rms(x, g):
    return x * lax.rsqrt(jnp.mean(x * x, axis=-1, keepdims=True) + NORM_EPS) * g


def _const_spec(shape, single_buffer=False):
    zeros = (0,) * len(shape)
    mode = pl.Buffered(1) if single_buffer else None
    return pl.BlockSpec(shape, lambda *_: zeros, pipeline_mode=mode)


def _mod_kernel(c_ref, w_ref, b_ref, o_ref):
    c = c_ref[...]
    o_ref[0] = _dot(_silu(c), w_ref[0], precision=HIGHEST) + b_ref[0]


def _adaln_mod(c, ada_w, ada_b):
    depth, d, n = ada_w.shape
    b = c.shape[0]
    tn = n // 4
    return pl.pallas_call(
        _mod_kernel,
        grid=(depth, n // tn),
        in_specs=[
            pl.BlockSpec((b, d), lambda l, j: (0, 0)),
            pl.BlockSpec((1, d, tn), lambda l, j: (l, 0, j)),
            pl.BlockSpec((1, 1, tn), lambda l, j: (l, 0, j)),
        ],
        out_specs=pl.BlockSpec((1, b, tn), lambda l, j: (l, 0, j)),
        out_shape=jax.ShapeDtypeStruct((depth, b, n), F32),
        compiler_params=_cparams("parallel", "parallel"),
        name="adaln_mod",
    )(c, ada_w, ada_b.reshape(depth, 1, n))


def _norm_mod(x, g, scale, shift):
    return _rms(x, g) * (1.0 + scale) + shift


FFN_COLS = 256


def _mix_ffn_kernel(o_ref, x_ref, mod_ref, ng_ref, wmix_ref, win_ref, wout_ref, out_ref,
                    h_scr, xmid_scr, acc_scr):
    m = mod_ref[0]
    ng = ng_ref[...]
    fh = wout_ref.shape[0]
    y = _dot(o_ref[0], wmix_ref[...])
    x = x_ref[0] + (1.0 + m[2:3]) * _rms(y, ng[1:2])
    h_scr[...] = _norm_mod(x, ng[2:3], m[4:5], m[3:4]).astype(BF16)
    xmid_scr[...] = x
    acc_scr[...] = jnp.zeros_like(acc_scr)
    for j in range(fh // FFN_COLS):
        cols = slice(j * FFN_COLS, (j + 1) * FFN_COLS)
        h = h_scr[...]
        gate = _dot(h, win_ref[:, cols])
        up = _dot(h, win_ref[:, fh + j * FFN_COLS:fh + (j + 1) * FFN_COLS])
        acc_scr[...] += _dot((_silu(gate) * up).astype(BF16), wout_ref[cols, :])
    out_ref[0] = xmid_scr[...] + (1.0 + m[5:6]) * _rms(acc_scr[...], ng[3:4])


def _mix_ffn(o, x, mod, ng, w_mix, w_in, w_out, tm):
    b, s, d = x.shape
    return pl.pallas_call(
        _mix_ffn_kernel,
        grid=(b, s // tm),
        in_specs=[
            pl.BlockSpec((1, tm, o.shape[-1]), lambda i, j: (i, j, 0)),
            pl.BlockSpec((1, tm, d), lambda i, j: (i, j, 0)),
            pl.BlockSpec((1, 6, d), lambda i, j: (i, 0, 0)),
            _const_spec(ng.shape),
            _const_spec(w_mix.shape, single_buffer=True),
            _const_spec(w_in.shape, single_buffer=True),
            _const_spec(w_out.shape, single_buffer=True),
        ],
        out_specs=pl.BlockSpec((1, tm, d), lambda i, j: (i, j, 0)),
        out_shape=jax.ShapeDtypeStruct(x.shape, F32),
        scratch_shapes=[pltpu.VMEM((tm, d), BF16), pltpu.VMEM((tm, d), F32),
                        pltpu.VMEM((tm, d), F32)],
        compiler_params=_cparams("parallel", "parallel"),
        name="mix_ffn",
    )(o, x, mod, ng, w_mix, w_in, w_out)


def _gdn_in_kernel(x_ref, mod_ref, ng_ref, w_ref, wgate_ref, alog_ref, dtb_ref,
                   proj_ref, gates_ref, h_scr):
    x = x_ref[0]
    m = mod_ref[0]
    tm = x.shape[0]
    h_scr[...] = _norm_mod(x, ng_ref[0:1], m[1:2], m[0:1]).astype(BF16)
    h = h_scr[...]
    n = w_ref.shape[1]
    tn = 512
    for j in range(n // tn):
        proj_ref[0, :, j * tn:(j + 1) * tn] = _dot(h, w_ref[:, j * tn:(j + 1) * tn]).astype(BF16)
    raw = _dot(h, wgate_ref[...])
    nh = raw.shape[1] // 2
    beta = jax.nn.sigmoid(raw[:, :nh])
    a = raw[:, nh:] + dtb_ref[...]
    softplus = jnp.maximum(a, 0.0) + jnp.log1p(jnp.exp(-jnp.abs(a)))
    g = -jnp.exp(alog_ref[...]) * softplus
    r = lax.broadcasted_iota(jnp.int32, (tm, tm), 0)
    c = lax.broadcasted_iota(jnp.int32, (tm, tm), 1)
    shift = GDN_CHUNK.bit_length() - 1
    same_chunk = jnp.right_shift(r, shift) == jnp.right_shift(c, shift)
    mcum = jnp.where(same_chunk, jnp.where(r >= c, 1.0, 0.0), 0.0).astype(BF16)
    g_hi = g.astype(BF16)
    g_mid = (g - g_hi.astype(F32)).astype(BF16)
    g_lo = (g - g_hi.astype(F32) - g_mid.astype(F32)).astype(BF16)
    parts = _dot(mcum, jnp.concatenate([g_hi, g_mid, g_lo], axis=1))
    gcum = parts[:, :nh] + (parts[:, nh:2 * nh] + parts[:, 2 * nh:])
    gates_ref[0, :, 0:nh] = beta
    gates_ref[0, :, nh:2 * nh] = gcum


def _gdn_in(x, mod, ng, w, wgate, a_log, dt_bias, tm):
    b, s, d = x.shape
    n = w.shape[1]
    nh2 = wgate.shape[1]
    return pl.pallas_call(
        _gdn_in_kernel,
        grid=(b, s // tm),
        in_specs=[
            pl.BlockSpec((1, tm, d), lambda i, j: (i, j, 0)),
            pl.BlockSpec((1, 6, d), lambda i, j: (i, 0, 0)),
            _const_spec(ng.shape),
            _const_spec(w.shape),
            _const_spec(wgate.shape),
            _const_spec(a_log.shape),
            _const_spec(dt_bias.shape),
        ],
        out_specs=[
            pl.BlockSpec((1, tm, n), lambda i, j: (i, j, 0)),
            pl.BlockSpec((1, tm, nh2), lambda i, j: (i, j, 0)),
        ],
        out_shape=[
            jax.ShapeDtypeStruct((b, s, n), BF16),
            jax.ShapeDtypeStruct((b, s, nh2), F32),
        ],
        scratch_shapes=[pltpu.VMEM((tm, d), BF16)],
        compiler_params=_cparams("parallel", "parallel"),
        name="gdn_in",
    )(x, mod, ng, w, wgate, a_log, dt_bias)


SOLVE_BASE = 8
GDN_CHUNKS_PER_STEP = 4


def _gdn_kernel(proj_ref, gcol_ref, grow_ref, convw_ref, ng_ref, shift_ref, out_ref,
                conv_scr, carry_scr, state_scr):
    tb = proj_ref.shape[1]
    nh = GDN_HEADS
    dk = GDN_DK
    cs = GDN_CHUNK
    hk = nh * dk
    nqkv = conv_scr.shape[1]

    @pl.when(pl.program_id(1) == 0)
    def _():
        carry_scr[...] = jnp.zeros_like(carry_scr)
        state_scr[...] = jnp.zeros_like(state_scr)

    cw = 512
    row8 = lax.broadcasted_iota(jnp.int32, (8, cw), 0)
    for j in range(nqkv // cw):
        cols = slice(j * cw, (j + 1) * cw)
        u = proj_ref[0, :, cols]
        shifted = _dot(shift_ref[...], u)
        prev = carry_scr[:, cols]
        w = convw_ref[:, cols]
        raw = u.astype(F32)
        y = w[GDN_CONV - 1:GDN_CONV] * raw
        head = y[0:8]
        for d in range(1, GDN_CONV):
            wd = w[GDN_CONV - 1 - d:GDN_CONV - d]
            y = y + wd * shifted[(d - 1) * tb:d * tb]
            head = head + wd * (shifted[(d - 1) * tb:(d - 1) * tb + 8]
                                + jnp.where(row8 < d, pltpu.roll(prev, d, axis=0), 0.0))
        conv_scr[:, cols] = _silu(y)
        conv_scr[0:8, cols] = _silu(head)
        carry_scr[:, cols] = raw[tb - 8:tb]

    ri = lax.broadcasted_iota(jnp.int32, (cs, cs), 0)
    ci = lax.broadcasted_iota(jnp.int32, (cs, cs), 1)
    lower = ri >= ci
    strict = ri > ci
    ng = ng_ref[...]

    heads = range(nh)
    eye = jnp.where(ri == ci, 1.0, 0.0)

    def same_block(width):
        shift = width.bit_length() - 1
        return jnp.right_shift(ri, shift) == jnp.right_shift(ci, shift)

    base_blocks = same_block(SOLVE_BASE)
    merge_masks = []
    width = SOLVE_BASE
    while width < cs:
        merge_masks.append((same_block(width), same_block(2 * width)))
        width *= 2

    nc = GDN_CHUNKS_PER_STEP
    units = [(cc, h) for cc in range(nc) for h in heads]
    every = range(len(units))

    def chunk_body(c, carry):
        qs, ks, rs, decays, qgs, kgs, glasts, rowss = [], [], [], [], [], [], [], []
        for cc, h in units:
            r0 = pl.multiple_of((c * nc + cc) * cs, cs)
            rows = pl.ds(r0, cs)
            gc = gcol_ref[0, rows, :]
            gr = grow_ref[0, c * nc + cc]
            rowss.append(rows)
            q = conv_scr[rows, h * dk:(h + 1) * dk]
            k = conv_scr[rows, hk + h * dk:hk + (h + 1) * dk]
            v = conv_scr[rows, 2 * hk + h * dk:2 * hk + (h + 1) * dk]
            q = q * (lax.rsqrt(jnp.sum(q * q, axis=-1, keepdims=True) + L2_EPS) * (dk ** -0.5))
            k = k * lax.rsqrt(jnp.sum(k * k, axis=-1, keepdims=True) + L2_EPS)
            beta = gc[:, h:h + 1]
            gcum = gc[:, nh + h:nh + h + 1]
            grow = gr[nh + h:nh + h + 1, :]
            glast = gcum[cs - 1:cs, :]
            eg = jnp.exp(gcum)
            kb = k * beta
            qs.append(jnp.concatenate([kb, q], axis=0).astype(BF16))
            ks.append(k.astype(BF16))
            rs.append(jnp.concatenate([v * beta, kb * eg], axis=1))
            decays.append(jnp.where(lower, jnp.exp(jnp.minimum(gcum - grow, 0.0)), 0.0))
            qgs.append((q * eg).astype(BF16))
            kgs.append((k * jnp.exp(glast - gcum)).astype(BF16))
            glasts.append(glast)
        lows, attns = [], []
        for u in every:
            x = _dot_nt(qs[u], ks[u])
            lows.append(jnp.where(strict, x[:cs] * decays[u], 0.0))
            attns.append((x[cs:] * decays[u]).astype(BF16))
        ts, pws = [], []
        for u in every:
            m = -jnp.where(base_blocks, lows[u], 0.0)
            ts.append(eye + m)
            pws.append(m)
        for j in range(1, (SOLVE_BASE - 1).bit_length()):
            for u in every:
                pws[u] = _bdot(pws[u], pws[u])
                ts[u] = ts[u] + _bdot(ts[u], pws[u])
        for inner, outer in merge_masks:
            for u in every:
                tb16 = ts[u].astype(BF16)
                coupling = jnp.where(outer, jnp.where(inner, 0.0, lows[u]), 0.0)
                ts[u] = ts[u] - _bdot(tb16, _dot(coupling.astype(BF16), tb16))
        for u in every:
            rs[u] = _bdot(ts[u], rs[u])
        for cc in range(nc):
            v_news, sts = [], []
            for h in heads:
                u = cc * nh + h
                st = state_scr[h]
                x = _dot(jnp.concatenate([rs[u][:, GDN_DV:].astype(BF16), qgs[u]], axis=0),
                         st.astype(BF16))
                v_news.append((rs[u][:, :GDN_DV] - x[:cs]).astype(BF16))
                qgs[u] = x[cs:]
                sts.append(st)
            for h in heads:
                u = cc * nh + h
                o = qgs[u] + _dot(attns[u], v_news[h])
                state_scr[h] = sts[h] * jnp.exp(glasts[u]) + _dot_tn(kgs[u], v_news[h])
                z = proj_ref[0, rowss[u], 3 * hk + h * dk:3 * hk + (h + 1) * dk].astype(F32)
                out_ref[0, rowss[u], h * dk:(h + 1) * dk] = (_rms(o, ng) * _silu(z)).astype(BF16)
        return carry

    lax.fori_loop(0, tb // (cs * nc), chunk_body, 0)


def _gdn_core(proj, gates, grow, conv_w, norm_g, tb):
    b, s, n = proj.shape
    hv = GDN_HEADS * GDN_DV
    nqkv = conv_w.shape[1]
    t_out = jnp.arange(tb)[:, None]
    t_in = jnp.arange(tb)[None, :]
    shift = jnp.concatenate([(t_out - d == t_in) for d in range(1, GDN_CONV)], axis=0).astype(BF16)
    return pl.pallas_call(
        _gdn_kernel,
        grid=(b, s // tb),
        in_specs=[
            pl.BlockSpec((1, tb, n), lambda i, j: (i, j, 0)),
            pl.BlockSpec((1, tb, gates.shape[-1]), lambda i, j: (i, j, 0)),
            pl.BlockSpec((1, tb // GDN_CHUNK) + grow.shape[2:], lambda i, j: (i, j, 0, 0)),
            _const_spec(conv_w.shape),
            _const_spec(norm_g.shape),
            _const_spec(shift.shape),
        ],
        out_specs=pl.BlockSpec((1, tb, hv), lambda i, j: (i, j, 0)),
        out_shape=jax.ShapeDtypeStruct((b, s, hv), BF16),
        scratch_shapes=[
            pltpu.VMEM((tb, nqkv), F32),
            pltpu.VMEM((8, nqkv), F32),
            pltpu.VMEM((GDN_HEADS, GDN_DK, GDN_DV), F32),
        ],
        compiler_params=_cparams("parallel", "arbitrary"),
        name="gdn_core",
    )(proj, gates, grow, conv_w, norm_g, shift)


def _dsa_in_kernel(x_ref, mod_ref, ng_ref, w_ref, qg_ref, kvg_ref, lng_ref, lnb_ref,
                   wuq_ref, wqi_ref, q_ref, qi_ref, ckv_ref, aux_ref):
    x = x_ref[0]
    m = mod_ref[0]
    h = _norm_mod(x, ng_ref[0:1], m[1:2], m[0:1]).astype(BF16)
    proj = _dot(h, w_ref[...])
    ql, kvl = DSA_Q_LORA, DSA_KV_LORA
    c_q = _rms(proj[:, :ql], qg_ref[...]).astype(BF16)
    ckv_ref[0] = _rms(proj[:, ql:ql + kvl], kvg_ref[...]).astype(BF16)
    q_ref[0] = _dot(c_q, wuq_ref[...]).astype(BF16)
    qi_ref[0] = _dot(c_q, wqi_ref[...]).astype(BF16)
    tail = proj[:, ql + kvl:ql + kvl + LANES]
    lane = lax.broadcasted_iota(jnp.int32, tail.shape, 1)
    is_key = lane < IDX_DIM
    mu = jnp.sum(jnp.where(is_key, tail, 0.0), axis=-1, keepdims=True) * (1.0 / IDX_DIM)
    dev = jnp.where(is_key, tail - mu, 0.0)
    var = jnp.sum(dev * dev, axis=-1, keepdims=True) * (1.0 / IDX_DIM)
    kn = dev * lax.rsqrt(var + NORM_EPS) * lng_ref[...] + lnb_ref[...]
    wscale = (IDX_HEADS ** -0.5) * (IDX_DIM ** -0.5)
    aux_ref[0] = jnp.where(is_key, kn, tail * wscale)


def _dsa_in(x, mod, ng, w, qg, kvg, lng, lnb, wuq, wqi, tm):
    b, s, d = x.shape
    outs = [
        (wuq.shape[1], BF16),
        (wqi.shape[1], BF16),
        (DSA_KV_LORA, BF16),
        (LANES, F32),
    ]
    return pl.pallas_call(
        _dsa_in_kernel,
        grid=(b, s // tm),
        in_specs=[
            pl.BlockSpec((1, tm, d), lambda i, j: (i, j, 0)),
            pl.BlockSpec((1, 6, d), lambda i, j: (i, 0, 0)),
            _const_spec(ng.shape),
            _const_spec(w.shape),
            _const_spec(qg.shape),
            _const_spec(kvg.shape),
            _const_spec(lng.shape),
            _const_spec(lnb.shape),
            _const_spec(wuq.shape),
            _const_spec(wqi.shape),
        ],
        out_specs=[pl.BlockSpec((1, tm, n), lambda i, j: (i, j, 0)) for n, _ in outs],
        out_shape=[jax.ShapeDtypeStruct((b, s, n), dt) for n, dt in outs],
        compiler_params=_cparams("parallel", "parallel"),
        name="dsa_in",
    )(x, mod, ng, w, qg, kvg, lng, lnb, wuq, wqi)


SEL_QUERIES = 256
SEL_KC = 512
COUNT_ROWS = 16


def _select_kernel(qi_ref, auxq_ref, auxk_ref, mask_ref, key_scr, jcut_scr, *, topk):
    i = pl.program_id(1)
    tq = qi_ref.shape[1]
    s = auxk_ref.shape[1]
    kc = min(SEL_KC, s)
    nkc = (i * tq + tq + kc - 1) // kc
    q = qi_ref[0]
    qstack = jnp.concatenate([q[:, h * IDX_DIM:(h + 1) * IDX_DIM] for h in range(IDX_HEADS)], axis=0)
    wt = auxq_ref[0].T[IDX_DIM:IDX_DIM + IDX_HEADS, :]
    t = i * tq + lax.broadcasted_iota(jnp.int32, (1, tq), 1)
    sub_pos = lax.broadcasted_iota(jnp.int32, (kc, 1), 0)

    def score_body(j, carry):
        c0 = pl.multiple_of(j * kc, kc)
        kx = auxk_ref[0, pl.ds(c0, kc), :][:, :IDX_DIM].astype(BF16)
        lg = _dot_nt(kx, qstack)
        sc = jnp.zeros((kc, tq), F32)
        for h in range(IDX_HEADS):
            sc = sc + wt[h:h + 1, :] * jnp.maximum(lg[:, h * tq:(h + 1) * tq], 0.0)
        bits = lax.bitcast_convert_type(sc, jnp.int32)
        key = bits ^ (lax.shift_right_arithmetic(bits, 31) & 0x7FFFFFFF)
        key = jnp.where(sc == 0.0, 0, key)
        key = jnp.where(c0 + sub_pos <= t, key, INT_MIN)
        key_scr[pl.ds(c0, kc), :] = key
        return carry

    lax.fori_loop(0, nkc, score_body, 0)

    def count(hits):
        def body(j, acc):
            c0 = pl.multiple_of(j * kc, kc)
            hit = hits(key_scr[pl.ds(c0, kc), :], c0 + sub_pos)
            return acc + jnp.sum(hit.reshape(kc // COUNT_ROWS, COUNT_ROWS, tq), axis=0)
        acc = lax.fori_loop(0, nkc, body, jnp.zeros((COUNT_ROWS, tq), F32))
        return jnp.sum(acc, axis=0, keepdims=True)

    kf = float(topk)

    tau = jnp.where(count(lambda key, pos: jnp.where(key >= 0, 1.0, 0.0)) >= kf, 0, INT_MIN).astype(jnp.int32)

    def bit_body(b, tau):
        cand = tau + lax.shift_left(jnp.int32(1), 30 - b)
        return jnp.where(count(lambda key, pos: jnp.where(key >= cand, 1.0, 0.0)) >= kf, cand, tau)

    tau = lax.fori_loop(0, 31, bit_body, tau)

    need = kf - count(lambda key, pos: jnp.where(key > tau, 1.0, 0.0))
    n_ge = count(lambda key, pos: jnp.where(key >= tau, 1.0, 0.0))
    jcut_scr[...] = jnp.full(jcut_scr.shape, s, jnp.int32)

    @pl.when(jnp.max(n_ge) > kf)
    def _():
        nbits = max(1, (s - 1).bit_length())

        def pos_body(b, lo):
            cand = lo + lax.shift_left(jnp.int32(1), nbits - 1 - b)
            below = count(lambda key, pos: jnp.where(key == tau, jnp.where(pos < cand, 1.0, 0.0), 0.0))
            return jnp.where(below < need, cand, lo)

        jcut_scr[...] = lax.fori_loop(0, nbits, pos_body, jnp.zeros((1, tq), jnp.int32))

    jcut = jcut_scr[...]

    def mask_body(j, carry):
        c0 = pl.multiple_of(j * kc, kc)
        key = key_scr[pl.ds(c0, kc), :]
        pos = c0 + sub_pos
        sel = jnp.where(key > tau, 1.0, jnp.where(key == tau, jnp.where(pos <= jcut, 1.0, 0.0), 0.0))
        sel = jnp.where(pos <= t, sel, 0.0)
        for g in range(kc // LANES):
            tile = sel[g * LANES:(g + 1) * LANES, :].T
            mask_ref[0, :, pl.ds(pl.multiple_of(c0 + g * LANES, LANES), LANES)] = tile.astype(jnp.int8)
        return carry

    lax.fori_loop(0, nkc, mask_body, 0)

    def zero_body(j, carry):
        c0 = pl.multiple_of(j * kc, kc)
        mask_ref[0, :, pl.ds(c0, kc)] = jnp.zeros((tq, kc), jnp.int8)
        return carry

    lax.fori_loop(nkc, s // kc, zero_body, 0)


def _dsa_select(qi, aux, topk):
    b, s, _ = qi.shape
    tq = min(SEL_QUERIES, s)
    return pl.pallas_call(
        functools.partial(_select_kernel, topk=topk),
        grid=(b, s // tq),
        in_specs=[
            pl.BlockSpec((1, tq, qi.shape[-1]), lambda i, j: (i, j, 0)),
            pl.BlockSpec((1, tq, LANES), lambda i, j: (i, j, 0)),
            pl.BlockSpec((1, s, LANES), lambda i, j: (i, 0, 0)),
        ],
        out_specs=pl.BlockSpec((1, tq, s), lambda i, j: (i, j, 0)),
        out_shape=jax.ShapeDtypeStruct((b, s, s), jnp.int8),
        scratch_shapes=[pltpu.VMEM((s, tq), jnp.int32), pltpu.VMEM((1, tq), jnp.int32)],
        compiler_params=_cparams("parallel", "parallel"),
        name="dsa_select",
    )(qi, aux, aux)


ATT_KB = 512
ATT_ROWS = 256
LOG2E = 1.4426950408889634


def _attn_kernel(q_ref, ckv_ref, mask_ref, wuk_ref, wuv_ref, o_ref,
                 qlat_scr, m_scr, l_scr, acc_scr):
    i = pl.program_id(1)
    tq = q_ref.shape[1]
    s = ckv_ref.shape[1]
    nh, dqk, dl = wuk_ref.shape
    dv = wuv_ref.shape[2]
    kb = min(ATT_KB, s)
    nkb = (i * tq + tq + kb - 1) // kb
    rg = ATT_ROWS
    hg = rg // tq

    q = q_ref[0]
    scale = (dqk ** -0.5) * LOG2E
    for h in range(nh):
        ql = _dot(q[:, h * dqk:(h + 1) * dqk], wuk_ref[h]) * scale
        qlat_scr[h * tq:(h + 1) * tq, :] = ql.astype(BF16)
    m_scr[...] = jnp.full(m_scr.shape, NEG_BIG, F32)
    l_scr[...] = jnp.zeros_like(l_scr)
    acc_scr[...] = jnp.zeros_like(acc_scr)
    n_groups = nh * tq // rg
    lane_tiles = kb // LANES

    def body(j, carry):
        c0 = pl.multiple_of(j * kb, kb)
        kv = ckv_ref[0, pl.ds(c0, kb), :]
        bias = ((mask_ref[0, :, pl.ds(c0, kb)].astype(F32) - 1.0) * (-NEG_BIG)).astype(BF16)

        def scores(g):
            return _dot_nt(qlat_scr[g * rg:(g + 1) * rg, :], kv).astype(BF16)

        sc_next = scores(0)
        for g in range(n_groups):
            rows = slice(g * rg, (g + 1) * rg)
            sc = sc_next
            if g + 1 < n_groups:
                sc_next = scores(g + 1)
            sc = (sc.reshape(hg, tq, kb) + bias[None]).reshape(rg, kb)
            part = sc[:, :LANES]
            for t in range(1, lane_tiles):
                part = jnp.maximum(part, sc[:, t * LANES:(t + 1) * LANES])
            m_old = m_scr[rows, :]
            m_new = jnp.maximum(m_old, jnp.max(part.astype(F32), axis=-1, keepdims=True))
            alpha = jnp.exp2(m_old - m_new)
            p = jnp.exp2(sc - jnp.tile(m_new.astype(BF16), (1, lane_tiles)))
            tiles = [p[:, t * LANES:(t + 1) * LANES] for t in range(lane_tiles)]
            while len(tiles) > 1:
                tiles = [a + b for a, b in zip(tiles[::2], tiles[1::2])]
            l_scr[rows, :] = alpha * l_scr[rows, :] + tiles[0].astype(F32)
            acc_scr[rows, :] = (jnp.tile(alpha, (1, dl // LANES)) * acc_scr[rows, :] + _dot(p, kv))
            m_scr[rows, :] = m_new
        return carry

    def pair_body(jj, carry):
        return body(2 * jj + 1, body(2 * jj, carry))

    lax.fori_loop(0, nkb // 2, pair_body, 0)

    @pl.when(nkb % 2 == 1)
    def _():
        body(nkb - 1, 0)

    inv_l = 1.0 / jnp.sum(l_scr[...], axis=-1, keepdims=True)
    for h in range(nh):
        rows = slice(h * tq, (h + 1) * tq)
        o_h = _dot(acc_scr[rows, :].astype(BF16), wuv_ref[h]) * inv_l[rows]
        o_ref[0, :, h * dv:(h + 1) * dv] = o_h.astype(BF16)


def _dsa_attn(q, ckv, mask, wuk, wuv):
    b, s, _ = q.shape
    tq = Q_BLOCK
    nh, dqk, dl = wuk.shape
    dv = wuv.shape[2]
    return pl.pallas_call(
        _attn_kernel,
        grid=(b, s // tq),
        in_specs=[
            pl.BlockSpec((1, tq, q.shape[-1]), lambda i, j: (i, j, 0)),
            pl.BlockSpec((1, s, dl), lambda i, j: (i, 0, 0)),
            pl.BlockSpec((1, tq, s), lambda i, j: (i, j, 0)),
            _const_spec(wuk.shape),
            _const_spec(wuv.shape),
        ],
        out_specs=pl.BlockSpec((1, tq, nh * dv), lambda i, j: (i, j, 0)),
        out_shape=jax.ShapeDtypeStruct((b, s, nh * dv), BF16),
        scratch_shapes=[
            pltpu.VMEM((nh * tq, dl), BF16),
            pltpu.VMEM((nh * tq, LANES), F32),
            pltpu.VMEM((nh * tq, LANES), F32),
            pltpu.VMEM((nh * tq, dl), F32),
        ],
        compiler_params=_cparams("parallel", "parallel"),
        name="dsa_attn",
    )(q, ckv, mask, wuk, wuv)


def _gdn_mixer(x, mod, ng, w_in, conv_w, a_log, dt_bias, norm_g, tm):
    b, s, _ = x.shape
    nmain = 2 * GDN_HEADS * GDN_DK + 2 * GDN_HEADS * GDN_DV
    proj, gates = _gdn_in(
        x, mod, ng, w_in[:, :nmain].astype(BF16), w_in[:, nmain:].astype(BF16),
        a_log.reshape(1, -1), dt_bias.reshape(1, -1), tm)
    grow = jnp.swapaxes(gates.reshape(b, s // GDN_CHUNK, GDN_CHUNK, -1), 2, 3)
    return _gdn_core(proj, gates, grow, conv_w, norm_g.reshape(1, -1), min(tm, 256))


def _dsa_mixer(x, mod, ng, w_in, qg, kvg, wuq, wuk, wuv, wqi, lng, lnb, tm):
    b, s, d = x.shape
    topk = min(TOPK_MAX, s // 4)
    n_in = w_in.shape[1]
    n_pad = DSA_Q_LORA + DSA_KV_LORA + LANES
    w_in_p = jnp.pad(w_in, ((0, 0), (0, n_pad - n_in))).astype(BF16)
    pad = LANES - IDX_DIM
    q, qi, ckv, aux = _dsa_in(
        x, mod, ng, w_in_p, qg.reshape(1, -1), kvg.reshape(1, -1),
        jnp.pad(lng, (0, pad)).reshape(1, -1), jnp.pad(lnb, (0, pad)).reshape(1, -1),
        wuq.astype(BF16), wqi.astype(BF16), tm)
    mask = _dsa_select(qi, aux, topk)
    return _dsa_attn(q, ckv, mask, wuk.astype(BF16), wuv.astype(BF16))


def kernel(x, c, ada_w, ada_b, norm_g, ffn_w_in, ffn_w_out, gdn_w_in, gdn_conv_w, gdn_a_log,
           gdn_dt_bias, gdn_norm_g, gdn_w_out, dsa_w_in, dsa_q_norm_g, dsa_kv_norm_g, dsa_w_uq,
           dsa_w_uk, dsa_w_uv, dsa_w_qidx, dsa_kidx_ln_g, dsa_kidx_ln_b, dsa_w_out):
    b, s, d = x.shape
    depth = ada_w.shape[0]
    tm = min(512, s)
    mods = _adaln_mod(c, ada_w, ada_b).reshape(depth, b, 6, d)
    for layer in range(depth):
        mod = mods[layer]
        ng = norm_g[layer]
        j = layer // 2
        if layer % 2 == 0:
            o = _gdn_mixer(x, mod, ng, gdn_w_in[j], gdn_conv_w[j], gdn_a_log[j], gdn_dt_bias[j],
                           gdn_norm_g[j], tm)
            w_mix = gdn_w_out[j]
        else:
            o = _dsa_mixer(x, mod, ng, dsa_w_in[j], dsa_q_norm_g[j], dsa_kv_norm_g[j],
                           dsa_w_uq[j], dsa_w_uk[j], dsa_w_uv[j], dsa_w_qidx[j],
                           dsa_kidx_ln_g[j], dsa_kidx_ln_b[j], tm)
            w_mix = dsa_w_out[j]
        x = _mix_ffn(o, x, mod, ng, w_mix.astype(BF16), ffn_w_in[layer].astype(BF16),
                     ffn_w_out[layer].astype(BF16), tm)
    return x
```
